```python
import jax, jax.numpy as jnp
from jax import lax
import numpy as np

D_MODEL = 1024
BATCH = 8
SEQ = 4096
DEPTH = 4
DEC_BATCH = 8
DEC_SEQ = 16
PAST_LEN = 2048

CHUNK = 64
N_PREV_CHUNKS = 8
WINDOW = N_PREV_CHUNKS * CHUNK
BAND = (N_PREV_CHUNKS + 1) * CHUNK
N_HEADS = 8
HEAD_DIM = 64
ATTN_W = N_HEADS * HEAD_DIM
CONV_W = 512
CONV_K = 3
REL_CLIP = 128
N_REL = 2 * REL_CLIP + 1
D_FF = 2816
EPS = 1e-6
NEG_INF = -1e30
IN_COLS = 3 * CONV_W + 3 * ATTN_W + 2 * D_MODEL

kernel_name = "hybrid_shortconv_chunkband_macaron_step"


def _rmsnorm(x, g):
    xf = x.astype(jnp.float32)
    y = xf * lax.rsqrt(jnp.mean(xf * xf, axis=-1, keepdims=True) + EPS)
    return y.astype(x.dtype) * g


def _swiglu(h, w_gu, w_down):
    gu = h @ w_gu
    g, u = jnp.split(gu, 2, axis=-1)
    return (jax.nn.silu(g) * u) @ w_down


def _short_conv(cb, cc, cv, conv_prev, w_conv):
    u = cc * cv
    up = jnp.concatenate([conv_prev, u], axis=1)
    t = u.shape[1]
    y = w_conv[0] * up[:, 0:t]
    for j in range(1, CONV_K):
        y = y + w_conv[j] * up[:, j:j + t]
    return cb * y, up[:, -(CONV_K - 1):]


def _band_attention(q, k, v, q_pos, k_pos, rel_bias):
    s = jnp.einsum('bqhd,bkhd->bhqk', q, k).astype(jnp.float32) * (HEAD_DIM ** -0.5)
    dist = q_pos[:, None] - k_pos[None, :]
    idx = jnp.clip(dist, -REL_CLIP, REL_CLIP) + REL_CLIP
    s = s + rel_bias[:, idx].astype(jnp.float32)[None]
    qc = jnp.floor_divide(q_pos, CHUNK)[:, None]
    kc = jnp.floor_divide(k_pos, CHUNK)[None, :]
    valid = (k_pos[None, :] >= 0) & (kc <= qc) & (kc >= qc - N_PREV_CHUNKS)
    s = jnp.where(valid[None, None], s, NEG_INF)
    p = jax.nn.softmax(s, axis=-1).astype(v.dtype)
    return jnp.einsum('bhqk,bkhd->bqhd', p, v)


def _prompt_attention(q, k, v, rel_bias):
    b, s, h, d = q.shape
    nc = s // CHUNK
    pad = N_PREV_CHUNKS * CHUNK
    kp = jnp.pad(k, ((0, 0), (pad, 0), (0, 0), (0, 0)))
    vp = jnp.pad(v, ((0, 0), (pad, 0), (0, 0), (0, 0)))

    def one_chunk(c):
        qs = c * CHUNK
        qb = lax.dynamic_slice_in_dim(q, qs, CHUNK, axis=1)
        kb = lax.dynamic_slice_in_dim(kp, qs, BAND, axis=1)
        vb = lax.dynamic_slice_in_dim(vp, qs, BAND, axis=1)
        q_pos = qs + jnp.arange(CHUNK, dtype=jnp.int32)
        k_pos = qs - pad + jnp.arange(BAND, dtype=jnp.int32)
        return _band_attention(qb, kb, vb, q_pos, k_pos, rel_bias)

    out = lax.map(one_chunk, jnp.arange(nc, dtype=jnp.int32))
    out = jnp.transpose(out, (1, 0, 2, 3, 4)).reshape(b, s, h, d)
    rows = min(WINDOW, s)
    return out, k[:, -rows:], v[:, -rows:]


def _sample_attention(q, k, v, ck, cv, rel_bias):
    t = q.shape[1]
    r = ck.shape[1]
    k_all = jnp.concatenate([ck, k], axis=1)
    v_all = jnp.concatenate([cv, v], axis=1)
    q_pos = PAST_LEN + jnp.arange(t, dtype=jnp.int32)
    k_pos = jnp.concatenate([PAST_LEN - r + jnp.arange(r, dtype=jnp.int32), q_pos])
    out = _band_attention(q, k_all, v_all, q_pos, k_pos, rel_bias)
    return out, k, v


def _mixer(h, conv_prev, attn_fn, w_in, b_gate, w_conv, w_conv_out, w_attn_out, w_o):
    b, t, _ = h.shape
    z = h @ w_in
    splits = [CONV_W, 2 * CONV_W, 3 * CONV_W, 3 * CONV_W + ATTN_W,
              3 * CONV_W + 2 * ATTN_W, 3 * CONV_W + 3 * ATTN_W]
    cb, cc, cv, q, k, v, g = jnp.split(z, splits, axis=-1)
    g = jax.nn.sigmoid(g + b_gate)
    g_conv, g_attn = jnp.split(g, 2, axis=-1)
    y_conv, conv_new = _short_conv(cb, cc, cv, conv_prev, w_conv)
    q = q.reshape(b, t, N_HEADS, HEAD_DIM)
    k = k.reshape(b, t, N_HEADS, HEAD_DIM)
    v = v.reshape(b, t, N_HEADS, HEAD_DIM)
    y_attn, k_new, v_new = attn_fn(q, k, v)
    y_attn = y_attn.reshape(b, t, ATTN_W)
    m = g_conv * (y_conv @ w_conv_out) + g_attn * (y_attn @ w_attn_out)
    return m @ w_o, conv_new, k_new, v_new


def setup_inputs(seed: int = 0) -> dict:
    key = jax.random.key(seed)
    ks = jax.random.split(key, 24)
    f32 = jnp.float32
    kv_rows = min(WINDOW, PAST_LEN)

    def nrm(k, shape, scale):
        return jax.random.normal(k, shape, f32) * scale

    def gain(k, shape):
        return 1.0 + 0.02 * jax.random.normal(k, shape, f32)

    return {
        "x_prompt": nrm(ks[0], (BATCH, SEQ, D_MODEL), 1.0),
        "x_sample": nrm(ks[1], (DEC_BATCH, DEC_SEQ, D_MODEL), 1.0),
        "cache_k": nrm(ks[2], (DEPTH, DEC_BATCH, kv_rows, N_HEADS, HEAD_DIM), 1.0),
        "cache_v": nrm(ks[3], (DEPTH, DEC_BATCH, kv_rows, N_HEADS, HEAD_DIM), 1.0),
        "state_conv": nrm(ks[4], (DEPTH, DEC_BATCH, CONV_K - 1, CONV_W), 1.0),
        "norm_ffn1": gain(ks[5], (DEPTH, D_MODEL)),
        "w_ffn1_gu": nrm(ks[6], (DEPTH, D_MODEL, 2 * D_FF), D_MODEL ** -0.5),
        "w_ffn1_down": nrm(ks[7], (DEPTH, D_FF, D_MODEL), D_FF ** -0.5),
        "norm_mix": gain(ks[8], (DEPTH, D_MODEL)),
        "w_in": nrm(ks[9], (DEPTH, D_MODEL, IN_COLS), D_MODEL ** -0.5),
        "b_gate": nrm(ks[10], (DEPTH, 2 * D_MODEL), 0.02),
        "w_conv": nrm(ks[11], (DEPTH, CONV_K, CONV_W), CONV_K ** -0.5),
        "rel_bias": nrm(ks[12], (DEPTH, N_HEADS, N_REL), 0.1),
        "w_conv_out": nrm(ks[13], (DEPTH, CONV_W, D_MODEL), CONV_W ** -0.5),
        "w_attn_out": nrm(ks[14], (DEPTH, ATTN_W, D_MODEL), ATTN_W ** -0.5),
        "w_o": nrm(ks[15], (DEPTH, D_MODEL, D_MODEL), D_MODEL ** -0.5),
        "norm_ffn2": gain(ks[16], (DEPTH, D_MODEL)),
        "w_ffn2_gu": nrm(ks[17], (DEPTH, D_MODEL, 2 * D_FF), D_MODEL ** -0.5),
        "w_ffn2_down": nrm(ks[18], (DEPTH, D_FF, D_MODEL), D_FF ** -0.5),
        "norm_final": gain(ks[19], (D_MODEL,)),
    }


def reference(x_prompt, x_sample, cache_k, cache_v, state_conv,
              norm_ffn1, w_ffn1_gu, w_ffn1_down, norm_mix, w_in, b_gate, w_conv,
              rel_bias, w_conv_out, w_attn_out, w_o, norm_ffn2, w_ffn2_gu, w_ffn2_down,
              norm_final):
    xp = x_prompt
    xs = x_sample
    conv_prev_p = jnp.zeros((xp.shape[0], CONV_K - 1, CONV_W), xp.dtype)
    kp_list, vp_list, cp_list = [], [], []
    ks_list, vs_list, cs_list = [], [], []
    for l in range(DEPTH):
        rb = rel_bias[l]
        xp = xp + 0.5 * _swiglu(_rmsnorm(xp, norm_ffn1[l]), w_ffn1_gu[l], w_ffn1_down[l])
        xs = xs + 0.5 * _swiglu(_rmsnorm(xs, norm_ffn1[l]), w_ffn1_gu[l], w_ffn1_down[l])
        mp, cnp, knp, vnp = _mixer(
            _rmsnorm(xp, norm_mix[l]), conv_prev_p,
            lambda q, k, v: _prompt_attention(q, k, v, rb),
            w_in[l], b_gate[l], w_conv[l], w_conv_out[l], w_attn_out[l], w_o[l])
        ck_l = cache_k[l]
        cv_l = cache_v[l]
        ms, cns, kns, vns = _mixer(
            _rmsnorm(xs, norm_mix[l]), state_conv[l],
            lambda q, k, v: _sample_attention(q, k, v, ck_l, cv_l, rb),
            w_in[l], b_gate[l], w_conv[l], w_conv_out[l], w_attn_out[l], w_o[l])
        xp = xp + mp
        xs = xs + ms
        xp = xp + 0.5 * _swiglu(_rmsnorm(xp, norm_ffn2[l]), w_ffn2_gu[l], w_ffn2_down[l])
        xs = xs + 0.5 * _swiglu(_rmsnorm(xs, norm_ffn2[l]), w_ffn2_gu[l], w_ffn2_down[l])
        kp_list.append(knp); vp_list.append(vnp); cp_list.append(cnp)
        ks_list.append(kns); vs_list.append(vns); cs_list.append(cns)
    y_prompt = _rmsnorm(xp, norm_final)
    y_sample = _rmsnorm(xs, norm_final)
    new_k_prompt = jnp.stack(kp_list, axis=0)
    new_v_prompt = jnp.stack(vp_list, axis=0)
    new_conv_prompt = jnp.stack(cp_list, axis=0)
    new_k_sample = jnp.stack(ks_list, axis=0)
    new_v_sample = jnp.stack(vs_list, axis=0)
    new_conv_sample = jnp.stack(cs_list, axis=0)
    return (y_prompt, y_sample, new_k_prompt, new_v_prompt, new_conv_prompt,
            new_k_sample, new_v_sample, new_conv_sample)
```

```python
import functools

import numpy as np
import jax
import jax.numpy as jnp
from jax import lax
from jax.experimental import pallas as pl
from jax.experimental.pallas import tpu as pltpu

CHUNK = 64
N_PREV_CHUNKS = 8
WINDOW = N_PREV_CHUNKS * CHUNK
BAND = WINDOW + CHUNK
N_HEADS = 8
HEAD_DIM = 64
ATTN_W = N_HEADS * HEAD_DIM
HEAD_PAIRS = N_HEADS // 2
PAIR_W = 2 * HEAD_DIM
REL_CLIP = 128
CONV_K = 3
PAST_LEN = 2048
EPS = 1e-6
NEG_INF = -1e30

F32 = jnp.float32
BF16 = jnp.bfloat16

V7X_SCOPED_VMEM_MAX_BYTES = 60000 * 1024
V7X_MXU_DIM = 256
F32_SUBLANES = 8

FFN_TOKEN_TILE = 512
FFN_CHUNK_MAX = 1024


def _resident(shape):
    zeros = (0,) * len(shape)
    return pl.BlockSpec(shape, lambda *_: zeros, pipeline_mode=pl.Buffered(1))


def _vmem_limit(estimate_bytes):
    return int(min(V7X_SCOPED_VMEM_MAX_BYTES, estimate_bytes))


def _rmsnorm(x, gain):
    return x * lax.rsqrt(jnp.mean(x * x, axis=-1, keepdims=True) + EPS) * gain


def _ff_chunks(d_ff):
    chunks = []
    left = d_ff
    while left > 0:
        ck = min(FFN_CHUNK_MAX, left)
        chunks.append(ck)
        left -= ck
    assert all(c % V7X_MXU_DIM == 0 for c in chunks), chunks
    return tuple(chunks)


def _ffn_body(x_ref, gain_ref, wgu_ref, wd_ref, gfin_ref, o_ref, *, chunks, d_ff, final_norm):
    x = x_ref[...]
    h = _rmsnorm(x, gain_ref[...]).astype(BF16)
    acc = None
    c0 = 0
    for ck in chunks:
        g = jnp.dot(h, wgu_ref[:, c0:c0 + ck], preferred_element_type=F32)
        u = jnp.dot(h, wgu_ref[:, d_ff + c0:d_ff + c0 + ck], preferred_element_type=F32)
        a = (jax.nn.silu(g) * u).astype(BF16)
        d = jnp.dot(a, wd_ref[c0:c0 + ck, :], preferred_element_type=F32)
        acc = d if acc is None else acc + d
        c0 += ck
    y = x + 0.5 * acc
    if final_norm:
        y = _rmsnorm(y, gfin_ref[...])
    o_ref[...] = y


def _ffn(x2d, gain, w_gu, w_down, gain_final, *, final_norm, name):
    n, d = x2d.shape
    d_ff = w_down.shape[0]
    tm = min(FFN_TOKEN_TILE, n)
    assert n % tm == 0 and tm % F32_SUBLANES == 0
    chunks = _ff_chunks(d_ff)
    ck = max(chunks)
    est = (2 * (w_gu.size + w_down.size)
           + 4 * tm * d * 4
           + tm * (2 * ck * 4 + ck * 2 + d * 2 + 3 * d * 4)
           + (8 << 20))
    body = functools.partial(_ffn_body, chunks=chunks, d_ff=d_ff, final_norm=final_norm)
    return pl.pallas_call(
        body,
        grid=(n // tm,),
        in_specs=[
            pl.BlockSpec((tm, d), lambda i: (i, 0)),
            _resident((1, d)),
            _resident(w_gu.shape),
            _resident(w_down.shape),
            _resident((1, d)),
        ],
        out_specs=pl.BlockSpec((tm, d), lambda i: (i, 0)),
        out_shape=jax.ShapeDtypeStruct((n, d), F32),
        compiler_params=pltpu.CompilerParams(
            dimension_semantics=("arbitrary",),
            vmem_limit_bytes=_vmem_limit(est)),
        name=name,
    )(x2d, gain, w_gu, w_down, gain_final)


def _pair_attention(q_lo, q_hi, k_band, v_band, bias, valid):
    rows = q_lo.shape[0]
    qs = jnp.concatenate([q_lo, q_hi], axis=0)
    s = lax.dot_general(qs, k_band, (((1,), (1,)), ((), ())), preferred_element_type=F32)
    s = s + bias
    if valid is not None:
        s = jnp.where(valid, s, NEG_INF)
    m = jnp.max(s, axis=-1, keepdims=True)
    e = jnp.exp(s - m)
    p = e * (1.0 / jnp.sum(e, axis=-1, keepdims=True))
    o = jnp.dot(p.astype(BF16), v_band, preferred_element_type=F32)
    lane = lax.broadcasted_iota(jnp.int32, (rows, PAIR_W), 1)
    return jnp.where(lane < HEAD_DIM, o[:rows], o[rows:])


def _split_pair_queries(q):
    lane = lax.broadcasted_iota(jnp.int32, q.shape, 1)
    even = (lane % PAIR_W) < HEAD_DIM
    zero = jnp.zeros_like(q)
    return jnp.where(even, q, zero).astype(BF16), jnp.where(even, zero, q).astype(BF16)


def _short_conv(ubuf_ref, u, w_conv_ref, rows):
    ubuf_ref[pl.ds(F32_SUBLANES, rows), :] = u
    y = w_conv_ref[0:1, :] * ubuf_ref[pl.ds(F32_SUBLANES - 2, rows), :]
    y = y + w_conv_ref[1:2, :] * ubuf_ref[pl.ds(F32_SUBLANES - 1, rows), :]
    return y + w_conv_ref[2:3, :] * u


def _mixer_prompt_body(x_ref, gain_ref, win_ref, bg_ref, wconv_ref, bias_ref, wco_ref, wao_ref,
                       wo_ref, xo_ref, ko_ref, vo_ref, co_ref,
                       kbuf, vbuf, qlo, qhi, yattn, ubuf, *, conv_w):
    tm = WINDOW
    d = x_ref.shape[-1]
    t = pl.program_id(1)

    @pl.when(t == 0)
    def _():
        kbuf[0:tm, :] = jnp.zeros((tm, ATTN_W), BF16)
        vbuf[0:tm, :] = jnp.zeros((tm, ATTN_W), BF16)
        ubuf[0:F32_SUBLANES, :] = jnp.zeros((F32_SUBLANES, conv_w), F32)

    x = x_ref[...]
    h = _rmsnorm(x, gain_ref[...]).astype(BF16)

    zc = jnp.dot(h, win_ref[:, 0:3 * conv_w], preferred_element_type=F32)
    u = zc[:, conv_w:2 * conv_w] * zc[:, 2 * conv_w:3 * conv_w]
    y = _short_conv(ubuf, u, wconv_ref, tm)
    yconv = (zc[:, 0:conv_w] * y).astype(BF16)
    tail = u[tm - (CONV_K - 1):tm, :]
    co_ref[...] = tail
    ubuf[F32_SUBLANES - (CONV_K - 1):F32_SUBLANES, :] = tail

    q0 = 3 * conv_w
    zq = jnp.dot(h, win_ref[:, q0:q0 + 3 * ATTN_W], preferred_element_type=F32)
    k = zq[:, ATTN_W:2 * ATTN_W]
    v = zq[:, 2 * ATTN_W:3 * ATTN_W]
    ko_ref[...] = k
    vo_ref[...] = v
    kbuf[tm:2 * tm, :] = k.astype(BF16)
    vbuf[tm:2 * tm, :] = v.astype(BF16)
    q_even, q_odd = _split_pair_queries(zq[:, 0:ATTN_W] * (HEAD_DIM ** -0.5))
    qlo[...] = q_even
    qhi[...] = q_odd

    first_valid_row = jnp.where(t > 0, 0, WINDOW)
    col = lax.broadcasted_iota(jnp.int32, (2 * CHUNK, BAND), 1)

    def chunk_body(c, carry):
        r0 = pl.multiple_of(c * CHUNK, CHUNK)
        valid = (col + r0) >= first_valid_row
        for j in range(HEAD_PAIRS):
            lanes = slice(j * PAIR_W, (j + 1) * PAIR_W)
            out = _pair_attention(
                qlo[pl.ds(r0, CHUNK), lanes], qhi[pl.ds(r0, CHUNK), lanes],
                kbuf[pl.ds(r0, BAND), lanes], vbuf[pl.ds(r0, BAND), lanes],
                bias_ref[j], valid)
            yattn[pl.ds(r0, CHUNK), lanes] = out.astype(BF16)
        return carry

    lax.fori_loop(0, tm // CHUNK, chunk_body, 0)
    kbuf[0:tm, :] = kbuf[tm:2 * tm, :]
    vbuf[0:tm, :] = vbuf[tm:2 * tm, :]

    g0 = q0 + 3 * ATTN_W
    gate = jax.nn.sigmoid(jnp.dot(h, win_ref[:, g0:g0 + 2 * d], preferred_element_type=F32)
                          + bg_ref[...])
    mc = jnp.dot(yconv, wco_ref[...], preferred_element_type=F32)
    ma = jnp.dot(yattn[...], wao_ref[...], preferred_element_type=F32)
    m = gate[:, 0:d] * mc + gate[:, d:2 * d] * ma
    xo_ref[...] = x + jnp.dot(m.astype(BF16), wo_ref[...], preferred_element_type=F32)


def _mixer_prompt(x, gain, w_in, b_gate, w_conv, bias, w_co, w_ao, w_o, *, name):
    b, s, d = x.shape
    conv_w = w_conv.shape[-1]
    tm = WINDOW
    assert s % tm == 0 and conv_w % 128 == 0
    weights = 2 * (w_in.size + w_co.size + w_ao.size + w_o.size) + 4 * bias.size
    est = (weights
           + 4 * tm * d * 4
           + 4 * tm * ATTN_W * 4
           + (2 * 2 * tm + 3 * tm) * ATTN_W * 2
           + (tm + F32_SUBLANES) * conv_w * 4
           + tm * (3 * conv_w + 3 * ATTN_W + 4 * d + 3 * d) * 4
           + (8 << 20))
    row = lambda bi, ti: (bi, ti, 0)
    per_batch = lambda bi, ti: (bi, 0, 0)
    body = functools.partial(_mixer_prompt_body, conv_w=conv_w)
    return pl.pallas_call(
        body,
        grid=(b, s // tm),
        in_specs=[
            pl.BlockSpec((None, tm, d), row),
            _resident((1, d)),
            _resident(w_in.shape),
            _resident(b_gate.shape),
            _resident(w_conv.shape),
            _resident(bias.shape),
            _resident(w_co.shape),
            _resident(w_ao.shape),
            _resident(w_o.shape),
        ],
        out_specs=[
            pl.BlockSpec((None, tm, d), row),
            pl.BlockSpec((None, tm, ATTN_W), per_batch),
            pl.BlockSpec((None, tm, ATTN_W), per_batch),
            pl.BlockSpec((None, CONV_K - 1, conv_w), per_batch),
        ],
        out_shape=[
            jax.ShapeDtypeStruct((b, s, d), F32),
            jax.ShapeDtypeStruct((b, tm, ATTN_W), F32),
            jax.ShapeDtypeStruct((b, tm, ATTN_W), F32),
            jax.ShapeDtypeStruct((b, CONV_K - 1, conv_w), F32),
        ],
        scratch_shapes=[
            pltpu.VMEM((2 * tm, ATTN_W), BF16),
            pltpu.VMEM((2 * tm, ATTN_W), BF16),
            pltpu.VMEM((tm, ATTN_W), BF16),
            pltpu.VMEM((tm, ATTN_W), BF16),
            pltpu.VMEM((tm, ATTN_W), BF16),
            pltpu.VMEM((tm + F32_SUBLANES, conv_w), F32),
        ],
        compiler_params=pltpu.CompilerParams(
            dimension_semantics=("arbitrary", "arbitrary"),
            vmem_limit_bytes=_vmem_limit(est)),
        name=name,
    )(x, gain, w_in, b_gate, w_conv, bias, w_co, w_ao, w_o)


def _mixer_sample_body(x_ref, gain_ref, win_ref, bg_ref, wconv_ref, bias_ref, wco_ref, wao_ref,
                       wo_ref, ck_ref, cv_ref, st_ref, xo_ref, ko_ref, vo_ref, co_ref,
                       kband, vband, yconv, yattn, ubuf, *, conv_w, n_streams, t_new, all_valid):
    d = x_ref.shape[-1]
    r_cache = ck_ref.shape[1]
    x = x_ref[...]
    h = _rmsnorm(x, gain_ref[...]).astype(BF16)

    zc = jnp.dot(h, win_ref[:, 0:3 * conv_w], preferred_element_type=F32)
    u = zc[:, conv_w:2 * conv_w] * zc[:, 2 * conv_w:3 * conv_w]
    cb = zc[:, 0:conv_w]
    q0 = 3 * conv_w
    zq = jnp.dot(h, win_ref[:, q0:q0 + 3 * ATTN_W], preferred_element_type=F32)
    k = zq[:, ATTN_W:2 * ATTN_W]
    v = zq[:, 2 * ATTN_W:3 * ATTN_W]
    ko_ref[...] = k
    vo_ref[...] = v
    q_even, q_odd = _split_pair_queries(zq[:, 0:ATTN_W] * (HEAD_DIM ** -0.5))
    valid = None
    if not all_valid:
        shape = (2 * t_new, r_cache + t_new)
        q_chunk = (PAST_LEN + lax.broadcasted_iota(jnp.int32, shape, 0) % t_new) // CHUNK
        k_pos = PAST_LEN - r_cache + lax.broadcasted_iota(jnp.int32, shape, 1)
        k_chunk = lax.shift_right_arithmetic(k_pos, CHUNK.bit_length() - 1)
        valid = (k_pos >= 0) & (k_chunk <= q_chunk) & (k_chunk >= q_chunk - N_PREV_CHUNKS)

    for bi in range(n_streams):
        rows = slice(bi * t_new, (bi + 1) * t_new)
        ubuf[F32_SUBLANES - (CONV_K - 1):F32_SUBLANES, :] = st_ref[bi]
        y = _short_conv(ubuf, u[rows], wconv_ref, t_new)
        yconv[rows, :] = (cb[rows] * y).astype(BF16)
        co_ref[bi] = ubuf[pl.ds(F32_SUBLANES + t_new - (CONV_K - 1), CONV_K - 1), :]

        kband[0:r_cache, :] = ck_ref[bi]
        vband[0:r_cache, :] = cv_ref[bi]
        kband[r_cache:r_cache + t_new, :] = k[rows].astype(BF16)
        vband[r_cache:r_cache + t_new, :] = v[rows].astype(BF16)
        for j in range(HEAD_PAIRS):
            lanes = slice(j * PAIR_W, (j + 1) * PAIR_W)
            out = _pair_attention(q_even[rows, lanes], q_odd[rows, lanes],
                                  kband[:, lanes], vband[:, lanes], bias_ref[j], valid)
            yattn[rows, lanes] = out.astype(BF16)

    g0 = q0 + 3 * ATTN_W
    gate = jax.nn.sigmoid(jnp.dot(h, win_ref[:, g0:g0 + 2 * d], preferred_element_type=F32)
                          + bg_ref[...])
    mc = jnp.dot(yconv[...], wco_ref[...], preferred_element_type=F32)
    ma = jnp.dot(yattn[...], wao_ref[...], preferred_element_type=F32)
    m = gate[:, 0:d] * mc + gate[:, d:2 * d] * ma
    xo_ref[...] = x + jnp.dot(m.astype(BF16), wo_ref[...], preferred_element_type=F32)


def _mixer_sample(x2d, gain, w_in, b_gate, w_conv, bias, w_co, w_ao, w_o, cache_k, cache_v,
                  state, all_valid, *, name):
    n, d = x2d.shape
    n_streams, r_cache, _ = cache_k.shape
    t_new = n // n_streams
    conv_w = w_conv.shape[-1]
    n_keys = r_cache + t_new
    assert t_new % 16 == 0 and r_cache % 16 == 0
    operands = (x2d, gain, w_in, b_gate, w_conv, bias, w_co, w_ao, w_o, cache_k, cache_v, state)
    est = (sum(a.size * a.dtype.itemsize for a in operands)
           + 2 * (n * d + 2 * n * ATTN_W + n_streams * (CONV_K - 1) * conv_w) * 4
           + (2 * n_keys + 2 * n) * ATTN_W * 2
           + n * (3 * conv_w + 3 * ATTN_W + 4 * d + 3 * d) * 4
           + (8 << 20))
    body = functools.partial(_mixer_sample_body, conv_w=conv_w, n_streams=n_streams,
                             t_new=t_new, all_valid=all_valid)
    full = lambda shape: pl.BlockSpec(shape, lambda i: (0,) * len(shape))
    return pl.pallas_call(
        body,
        grid=(1,),
        in_specs=[_resident(a.shape) for a in operands],
        out_specs=[
            full((n, d)),
            full((n, ATTN_W)),
            full((n, ATTN_W)),
            full((n_streams, CONV_K - 1, conv_w)),
        ],
        out_shape=[
            jax.ShapeDtypeStruct((n, d), F32),
            jax.ShapeDtypeStruct((n, ATTN_W), F32),
            jax.ShapeDtypeStruct((n, ATTN_W), F32),
            jax.ShapeDtypeStruct((n_streams, CONV_K - 1, conv_w), F32),
        ],
        scratch_shapes=[
            pltpu.VMEM((n_keys, ATTN_W), BF16),
            pltpu.VMEM((n_keys, ATTN_W), BF16),
            pltpu.VMEM((n, conv_w), BF16),
            pltpu.VMEM((n, ATTN_W), BF16),
            pltpu.VMEM((t_new + F32_SUBLANES, conv_w), F32),
        ],
        compiler_params=pltpu.CompilerParams(
            dimension_semantics=("arbitrary",),
            vmem_limit_bytes=_vmem_limit(est)),
        name=name,
    )(*operands)


def _bias_tiles(rel_bias_l, q_pos, k_pos):
    dist = q_pos[:, None] - k_pos[None, :]
    idx = np.clip(dist, -REL_CLIP, REL_CLIP) + REL_CLIP
    tiles = rel_bias_l[:, idx]
    h, q, k = tiles.shape
    return tiles.reshape(h // 2, 2 * q, k)


def _band_valid(q_pos, k_pos):
    qc = q_pos[:, None] // CHUNK
    kc = k_pos[None, :] // CHUNK
    return (k_pos[None, :] >= 0) & (kc <= qc) & (kc >= qc - N_PREV_CHUNKS)


def kernel(x_prompt, x_sample, cache_k, cache_v, state_conv, norm_ffn1, w_ffn1_gu, w_ffn1_down,
           norm_mix, w_in, b_gate, w_conv, rel_bias, w_conv_out, w_attn_out, w_o, norm_ffn2,
           w_ffn2_gu, w_ffn2_down, norm_final):
    depth = w_in.shape[0]
    b, s, d = x_prompt.shape
    nb, ts, _ = x_sample.shape
    r_cache = cache_k.shape[2]
    assert min(WINDOW, s) == WINDOW

    xp = x_prompt.reshape(b * s, d)
    xs = x_sample.reshape(nb * ts, d)
    w1gu, w1d = w_ffn1_gu.astype(BF16), w_ffn1_down.astype(BF16)
    w2gu, w2d = w_ffn2_gu.astype(BF16), w_ffn2_down.astype(BF16)
    win, wco, wao, wo = (w_in.astype(BF16), w_conv_out.astype(BF16), w_attn_out.astype(BF16),
                         w_o.astype(BF16))
    ck = cache_k.reshape(depth, nb, r_cache, ATTN_W).astype(BF16)
    cv = cache_v.reshape(depth, nb, r_cache, ATTN_W).astype(BF16)
    gfin = norm_final.reshape(1, d)

    qp_pos = WINDOW + np.arange(CHUNK)
    kp_pos = np.arange(BAND)
    qs_pos = PAST_LEN + np.arange(ts)
    ks_pos = np.concatenate([PAST_LEN - r_cache + np.arange(r_cache), qs_pos])
    all_valid_s = bool(_band_valid(qs_pos, ks_pos).all())

    outs = {n: [] for n in ("kp", "vp", "cp", "ks", "vs", "cs")}
    for l in range(depth):
        last = l == depth - 1
        g1 = norm_ffn1[l].reshape(1, d)
        gm = norm_mix[l].reshape(1, d)
        g2 = norm_ffn2[l].reshape(1, d)
        bg = b_gate[l].reshape(1, 2 * d)
        bias_p = _bias_tiles(rel_bias[l], qp_pos, kp_pos)
        bias_s = _bias_tiles(rel_bias[l], qs_pos, ks_pos)

        xp = _ffn(xp, g1, w1gu[l], w1d[l], gfin, final_norm=False, name=f"ffn1_prompt_l{l}")
        xs = _ffn(xs, g1, w1gu[l], w1d[l], gfin, final_norm=False, name=f"ffn1_sample_l{l}")

        xp3, kp, vp, cp = _mixer_prompt(xp.reshape(b, s, d), gm, win[l], bg, w_conv[l], bias_p,
                                        wco[l], wao[l], wo[l], name=f"mixer_prompt_l{l}")
        xp = xp3.reshape(b * s, d)
        xs, ks, vs, cs = _mixer_sample(xs, gm, win[l], bg, w_conv[l], bias_s, wco[l], wao[l],
                                       wo[l], ck[l], cv[l], state_conv[l], all_valid_s,
                                       name=f"mixer_sample_l{l}")

        xp = _ffn(xp, g2, w2gu[l], w2d[l], gfin, final_norm=last, name=f"ffn2_prompt_l{l}")
        xs = _ffn(xs, g2, w2gu[l], w2d[l], gfin, final_norm=last, name=f"ffn2_sample_l{l}")

        outs["kp"].append(kp.reshape(b, WINDOW, N_HEADS, HEAD_DIM))
        outs["vp"].append(vp.reshape(b, WINDOW, N_HEADS, HEAD_DIM))
        outs["cp"].append(cp)
        outs["ks"].append(ks.reshape(nb, ts, N_HEADS, HEAD_DIM))
        outs["vs"].append(vs.reshape(nb, ts, N_HEADS, HEAD_DIM))
        outs["cs"].append(cs)

    stack = lambda name: jnp.stack(outs[name], axis=0)
    return (xp.reshape(b, s, d), xs.reshape(nb, ts, d), stack("kp"), stack("vp"), stack("cp"),
            stack("ks"), stack("vs"), stack("cs"))
```

```python
import functools

import numpy as np
import jax
import jax.numpy as jnp
from jax import lax
from jax.experimental import pallas as pl
from jax.experimental.pallas import tpu as pltpu

CHUNK = 64
N_PREV_CHUNKS = 8
WINDOW = N_PREV_CHUNKS * CHUNK
BAND = WINDOW + CHUNK
N_HEADS = 8
HEAD_DIM = 64
ATTN_W = N_HEADS * HEAD_DIM
HEAD_PAIRS = N_HEADS // 2
PAIR_W = 2 * HEAD_DIM
REL_CLIP = 128
CONV_K = 3
PAST_LEN = 2048
EPS = 1e-6
NEG_INF = -1e30

F32 = jnp.float32
BF16 = jnp.bfloat16

V7X_SCOPED_VMEM_MAX_BYTES = 60000 * 1024
V7X_MXU_DIM = 256
F32_SUBLANES = 8

FFN_TOKEN_TILE = 512
FFN_CHUNK_MAX = 1024


def _resident(shape):
    zeros = (0,) * len(shape)
    return pl.BlockSpec(shape, lambda *_: zeros, pipeline_mode=pl.Buffered(1))


def _vmem_limit(estimate_bytes):
    return int(min(V7X_SCOPED_VMEM_MAX_BYTES, estimate_bytes))


def _rmsnorm(x, gain):
    return x * lax.rsqrt(jnp.mean(x * x, axis=-1, keepdims=True) + EPS) * gain


def _ff_chunks(d_ff):
    chunks = []
    left = d_ff
    while left > 0:
        ck = min(FFN_CHUNK_MAX, left)
        chunks.append(ck)
        left -= ck
    assert all(c % V7X_MXU_DIM == 0 for c in chunks), chunks
    return tuple(chunks)


def _ffn_body(x_ref, gain_ref, wgu_ref, wd_ref, gfin_ref, o_ref, *, chunks, d_ff, final_norm):
    x = x_ref[...]
    h = _rmsnorm(x, gain_ref[...]).astype(BF16)
    acc = None
    c0 = 0
    for ck in chunks:
        g = jnp.dot(h, wgu_ref[:, c0:c0 + ck], preferred_element_type=F32)
        u = jnp.dot(h, wgu_ref[:, d_ff + c0:d_ff + c0 + ck], preferred_element_type=F32)
        a = (jax.nn.silu(g) * u).astype(BF16)
        d = jnp.dot(a, wd_ref[c0:c0 + ck, :], preferred_element_type=F32)
        acc = d if acc is None else acc + d
        c0 += ck
    y = x + 0.5 * acc
    if final_norm:
        y = _rmsnorm(y, gfin_ref[...])
    o_ref[...] = y


def _ffn(x2d, gain, w_gu, w_down, gain_final, *, final_norm, name):
    n, d = x2d.shape
    d_ff = w_down.shape[0]
    tm = min(FFN_TOKEN_TILE, n)
    assert n % tm == 0 and tm % F32_SUBLANES == 0
    chunks = _ff_chunks(d_ff)
    ck = max(chunks)
    est = (2 * (w_gu.size + w_down.size)
           + 4 * tm * d * 4
           + tm * (2 * ck * 4 + ck * 2 + d * 2 + 3 * d * 4)
           + (8 << 20))
    body = functools.partial(_ffn_body, chunks=chunks, d_ff=d_ff, final_norm=final_norm)
    return pl.pallas_call(
        body,
        grid=(n // tm,),
        in_specs=[
            pl.BlockSpec((tm, d), lambda i: (i, 0)),
            _resident((1, d)),
            _resident(w_gu.shape),
            _resident(w_down.shape),
            _resident((1, d)),
        ],
        out_specs=pl.BlockSpec((tm, d), lambda i: (i, 0)),
        out_shape=jax.ShapeDtypeStruct((n, d), F32),
        compiler_params=pltpu.CompilerParams(
            dimension_semantics=("arbitrary",),
            vmem_limit_bytes=_vmem_limit(est)),
        name=name,
    )(x2d, gain, w_gu, w_down, gain_final)


def _pair_scores(q_lo, q_hi, k_band):
    qs = jnp.concatenate([q_lo, q_hi], axis=0)
    return lax.dot_general(qs, k_band, (((1,), (1,)), ((), ())), preferred_element_type=F32)


def _pair_softmax_pv(s, v_band, bias, valid):
    rows = s.shape[0] // 2
    s = s + bias
    if valid is not None:
        s = jnp.where(valid, s, NEG_INF)
    m = jnp.max(s, axis=-1, keepdims=True)
    e = jnp.exp(s - m)
    o = jnp.dot(e.astype(BF16), v_band, preferred_element_type=F32)
    o = o * (1.0 / jnp.sum(e, axis=-1, keepdims=True))
    lane = lax.broadcasted_iota(jnp.int32, (rows, PAIR_W), 1)
    return jnp.where(lane < HEAD_DIM, o[:rows], o[rows:])


def _split_pair_queries(q):
    lane = lax.broadcasted_iota(jnp.int32, q.shape, 1)
    even = (lane % PAIR_W) < HEAD_DIM
    zero = jnp.zeros_like(q)
    return jnp.where(even, q, zero).astype(BF16), jnp.where(even, zero, q).astype(BF16)


def _short_conv(ubuf_ref, u, w_conv_ref, rows):
    ubuf_ref[pl.ds(F32_SUBLANES, rows), :] = u
    y = w_conv_ref[0:1, :] * ubuf_ref[pl.ds(F32_SUBLANES - 2, rows), :]
    y = y + w_conv_ref[1:2, :] * ubuf_ref[pl.ds(F32_SUBLANES - 1, rows), :]
    return y + w_conv_ref[2:3, :] * u


def _mixer_prompt_body(x_ref, gain_ref, win_ref, bg_ref, wconv_ref, bias_ref, wco_ref, wao_ref,
                       wo_ref, xo_ref, ko_ref, vo_ref, co_ref,
                       kbuf, vbuf, qlo, qhi, yattn, ubuf, sscr, *, conv_w):
    tm = WINDOW
    d = x_ref.shape[-1]
    t = pl.program_id(1)

    @pl.when(t == 0)
    def _():
        kbuf[0:tm, :] = jnp.zeros((tm, ATTN_W), BF16)
        vbuf[0:tm, :] = jnp.zeros((tm, ATTN_W), BF16)
        ubuf[0:F32_SUBLANES, :] = jnp.zeros((F32_SUBLANES, conv_w), F32)

    x = x_ref[...]
    h = _rmsnorm(x, gain_ref[...]).astype(BF16)

    zc = jnp.dot(h, win_ref[:, 0:3 * conv_w], preferred_element_type=F32)
    u = zc[:, conv_w:2 * conv_w] * zc[:, 2 * conv_w:3 * conv_w]
    y = _short_conv(ubuf, u, wconv_ref, tm)
    yconv = (zc[:, 0:conv_w] * y).astype(BF16)
    tail = u[tm - (CONV_K - 1):tm, :]
    co_ref[...] = tail
    ubuf[F32_SUBLANES - (CONV_K - 1):F32_SUBLANES, :] = tail

    q0 = 3 * conv_w
    zq = jnp.dot(h, win_ref[:, q0:q0 + 3 * ATTN_W], preferred_element_type=F32)
    k = zq[:, ATTN_W:2 * ATTN_W]
    v = zq[:, 2 * ATTN_W:3 * ATTN_W]
    ko_ref[...] = k
    vo_ref[...] = v
    kbuf[tm:2 * tm, :] = k.astype(BF16)
    vbuf[tm:2 * tm, :] = v.astype(BF16)
    q_even, q_odd = _split_pair_queries(zq[:, 0:ATTN_W] * (HEAD_DIM ** -0.5))
    qlo[...] = q_even
    qhi[...] = q_odd

    first_valid_row = jnp.where(t > 0, 0, WINDOW)
    col = lax.broadcasted_iota(jnp.int32, (2 * CHUNK, BAND), 1)

    n_chunks = tm // CHUNK
    assert HEAD_PAIRS % 2 == 0

    def scores(r0, j):
        lanes = slice(j * PAIR_W, (j + 1) * PAIR_W)
        return _pair_scores(qlo[pl.ds(r0, CHUNK), lanes], qhi[pl.ds(r0, CHUNK), lanes],
                            kbuf[pl.ds(r0, BAND), lanes])

    sscr[0] = scores(0, 0)

    def chunk_body(c, carry):
        r0 = pl.multiple_of(c * CHUNK, CHUNK)
        r0_next = pl.multiple_of(jnp.minimum(c + 1, n_chunks - 1) * CHUNK, CHUNK)
        valid = (col + r0) >= first_valid_row
        for j in range(HEAD_PAIRS):
            lanes = slice(j * PAIR_W, (j + 1) * PAIR_W)
            if j + 1 < HEAD_PAIRS:
                sscr[(j + 1) % 2] = scores(r0, j + 1)
            else:
                sscr[(j + 1) % 2] = scores(r0_next, 0)
            out = _pair_softmax_pv(sscr[j % 2], vbuf[pl.ds(r0, BAND), lanes], bias_ref[j], valid)
            yattn[pl.ds(r0, CHUNK), lanes] = out.astype(BF16)
        return carry

    lax.fori_loop(0, n_chunks, chunk_body, 0)
    kbuf[0:tm, :] = kbuf[tm:2 * tm, :]
    vbuf[0:tm, :] = vbuf[tm:2 * tm, :]

    g0 = q0 + 3 * ATTN_W
    gate = jax.nn.sigmoid(jnp.dot(h, win_ref[:, g0:g0 + 2 * d], preferred_element_type=F32)
                          + bg_ref[...])
    mc = jnp.dot(yconv, wco_ref[...], preferred_element_type=F32)
    ma = jnp.dot(yattn[...], wao_ref[...], preferred_element_type=F32)
    m = gate[:, 0:d] * mc + gate[:, d:2 * d] * ma
    xo_ref[...] = x + jnp.dot(m.astype(BF16), wo_ref[...], preferred_element_type=F32)


def _mixer_prompt(x, gain, w_in, b_gate, w_conv, bias, w_co, w_ao, w_o, *, name):
    b, s, d = x.shape
    conv_w = w_conv.shape[-1]
    tm = WINDOW
    assert s % tm == 0 and conv_w % 128 == 0
    weights = 2 * (w_in.size + w_co.size + w_ao.size + w_o.size) + 4 * bias.size
    est = (weights
           + 4 * tm * d * 4
           + 4 * tm * ATTN_W * 4
           + (2 * 2 * tm + 3 * tm) * ATTN_W * 2
           + (tm + F32_SUBLANES) * conv_w * 4
           + tm * (3 * conv_w + 3 * ATTN_W + 4 * d + 3 * d) * 4
           + (8 << 20))
    row = lambda bi, ti: (bi, ti, 0)
    per_batch = lambda bi, ti: (bi, 0, 0)
    body = functools.partial(_mixer_prompt_body, conv_w=conv_w)
    return pl.pallas_call(
        body,
        grid=(b, s // tm),
        in_specs=[
            pl.BlockSpec((None, tm, d), row),
            _resident((1, d)),
            _resident(w_in.shape),
            _resident(b_gate.shape),
            _resident(w_conv.shape),
            _resident(bias.shape),
            _resident(w_co.shape),
            _resident(w_ao.shape),
            _resident(w_o.shape),
        ],
        out_specs=[
            pl.BlockSpec((None, tm, d), row),
            pl.BlockSpec((None, tm, ATTN_W), per_batch),
            pl.BlockSpec((None, tm, ATTN_W), per_batch),
            pl.BlockSpec((None, CONV_K - 1, conv_w), per_batch),
        ],
        out_shape=[
            jax.ShapeDtypeStruct((b, s, d), F32),
            jax.ShapeDtypeStruct((b, tm, ATTN_W), F32),
            jax.ShapeDtypeStruct((b, tm, ATTN_W), F32),
            jax.ShapeDtypeStruct((b, CONV_K - 1, conv_w), F32),
        ],
        scratch_shapes=[
            pltpu.VMEM((2 * tm, ATTN_W), BF16),
            pltpu.VMEM((2 * tm, ATTN_W), BF16),
            pltpu.VMEM((tm, ATTN_W), BF16),
            pltpu.VMEM((tm, ATTN_W), BF16),
            pltpu.VMEM((tm, ATTN_W), BF16),
            pltpu.VMEM((tm + F32_SUBLANES, conv_w), F32),
            pltpu.VMEM((2, 2 * CHUNK, BAND), F32),
        ],
        compiler_params=pltpu.CompilerParams(
            dimension_semantics=("arbitrary", "arbitrary"),
            vmem_limit_bytes=_vmem_limit(est)),
        name=name,
    )(x, gain, w_in, b_gate, w_conv, bias, w_co, w_ao, w_o)


def _mixer_sample_body(x_ref, gain_ref, win_ref, bg_ref, wconv_ref, bias_ref, wco_ref, wao_ref,
                       wo_ref, ck_ref, cv_ref, st_ref, xo_ref, ko_ref, vo_ref, co_ref,
                       kband, vband, yconv, yattn, ubuf, *, conv_w, n_streams, t_new, all_valid):
    d = x_ref.shape[-1]
    r_cache = ck_ref.shape[1]
    x = x_ref[...]
    h = _rmsnorm(x, gain_ref[...]).astype(BF16)

    zc = jnp.dot(h, win_ref[:, 0:3 * conv_w], preferred_element_type=F32)
    u = zc[:, conv_w:2 * conv_w] * zc[:, 2 * conv_w:3 * conv_w]
    cb = zc[:, 0:conv_w]
    q0 = 3 * conv_w
    zq = jnp.dot(h, win_ref[:, q0:q0 + 3 * ATTN_W], preferred_element_type=F32)
    k = zq[:, ATTN_W:2 * ATTN_W]
    v = zq[:, 2 * ATTN_W:3 * ATTN_W]
    ko_ref[...] = k
    vo_ref[...] = v
    q_even, q_odd = _split_pair_queries(zq[:, 0:ATTN_W] * (HEAD_DIM ** -0.5))
    valid = None
    if not all_valid:
        shape = (2 * t_new, r_cache + t_new)
        q_chunk = (PAST_LEN + lax.broadcasted_iota(jnp.int32, shape, 0) % t_new) // CHUNK
        k_pos = PAST_LEN - r_cache + lax.broadcasted_iota(jnp.int32, shape, 1)
        k_chunk = lax.shift_right_arithmetic(k_pos, CHUNK.bit_length() - 1)
        valid = (k_pos >= 0) & (k_chunk <= q_chunk) & (k_chunk >= q_chunk - N_PREV_CHUNKS)

    for bi in range(n_streams):
        rows = slice(bi * t_new, (bi + 1) * t_new)
        ubuf[F32_SUBLANES - (CONV_K - 1):F32_SUBLANES, :] = st_ref[bi]
        y = _short_conv(ubuf, u[rows], wconv_ref, t_new)
        yconv[rows, :] = (cb[rows] * y).astype(BF16)
        co_ref[bi] = ubuf[pl.ds(F32_SUBLANES + t_new - (CONV_K - 1), CONV_K - 1), :]

        kband[0:r_cache, :] = ck_ref[bi]
        vband[0:r_cache, :] = cv_ref[bi]
        kband[r_cache:r_cache + t_new, :] = k[rows].astype(BF16)
        vband[r_cache:r_cache + t_new, :] = v[rows].astype(BF16)
        for j in range(HEAD_PAIRS):
            lanes = slice(j * PAIR_W, (j + 1) * PAIR_W)
            s = _pair_scores(q_even[rows, lanes], q_odd[rows, lanes], kband[:, lanes])
            out = _pair_softmax_pv(s, vband[:, lanes], bias_ref[j], valid)
            yattn[rows, lanes] = out.astype(BF16)

    g0 = q0 + 3 * ATTN_W
    gate = jax.nn.sigmoid(jnp.dot(h, win_ref[:, g0:g0 + 2 * d], preferred_element_type=F32)
                          + bg_ref[...])
    mc = jnp.dot(yconv[...], wco_ref[...], preferred_element_type=F32)
    ma = jnp.dot(yattn[...], wao_ref[...], preferred_element_type=F32)
    m = gate[:, 0:d] * mc + gate[:, d:2 * d] * ma
    xo_ref[...] = x + jnp.dot(m.astype(BF16), wo_ref[...], preferred_element_type=F32)


def _mixer_sample(x2d, gain, w_in, b_gate, w_conv, bias, w_co, w_ao, w_o, cache_k, cache_v,
                  state, all_valid, *, name):
    n, d = x2d.shape
    n_streams, r_cache, _ = cache_k.shape
    t_new = n // n_streams
    conv_w = w_conv.shape[-1]
    n_keys = r_cache + t_new
    assert t_new % 16 == 0 and r_cache % 16 == 0
    operands = (x2d, gain, w_in, b_gate, w_conv, bias, w_co, w_ao, w_o, cache_k, cache_v, state)
    est = (sum(a.size * a.dtype.itemsize for a in operands)
           + 2 * (n * d + 2 * n * ATTN_W + n_streams * (CONV_K - 1) * conv_w) * 4
           + (2 * n_keys + 2 * n) * ATTN_W * 2
           + n * (3 * conv_w + 3 * ATTN_W + 4 * d + 3 * d) * 4
           + (8 << 20))
    body = functools.partial(_mixer_sample_body, conv_w=conv_w, n_streams=n_streams,
                             t_new=t_new, all_valid=all_valid)
    full = lambda shape: pl.BlockSpec(shape, lambda i: (0,) * len(shape))
    return pl.pallas_call(
        body,
        grid=(1,),
        in_specs=[_resident(a.shape) for a in operands],
        out_specs=[
            full((n, d)),
            full((n, ATTN_W)),
            full((n, ATTN_W)),
            full((n_streams, CONV_K - 1, conv_w)),
        ],
        out_shape=[
            jax.ShapeDtypeStruct((n, d), F32),
            jax.ShapeDtypeStruct((n, ATTN_W), F32),
            jax.ShapeDtypeStruct((n, ATTN_W), F32),
            jax.ShapeDtypeStruct((n_streams, CONV_K - 1, conv_w), F32),
        ],
        scratch_shapes=[
            pltpu.VMEM((n_keys, ATTN_W), BF16),
            pltpu.VMEM((n_keys, ATTN_W), BF16),
            pltpu.VMEM((n, conv_w), BF16),
            pltpu.VMEM((n, ATTN_W), BF16),
            pltpu.VMEM((t_new + F32_SUBLANES, conv_w), F32),
        ],
        compiler_params=pltpu.CompilerParams(
            dimension_semantics=("arbitrary",),
            vmem_limit_bytes=_vmem_limit(est)),
        name=name,
    )(*operands)


def _band_bias(rel_bias_l, offset, n_q, n_k):
    period = n_k + n_q
    m = np.arange(period)
    m = np.where(m < n_k, m, m - period)
    idx = np.clip(offset - m, -REL_CLIP, REL_CLIP) + REL_CLIP
    vec = rel_bias_l[:, idx]
    h = vec.shape[0]
    flat = jnp.tile(vec, (1, n_q))[:, :n_q * (period - 1)]
    tiles = flat.reshape(h, n_q, period - 1)[:, :, :n_k]
    return tiles.reshape(h // 2, 2 * n_q, n_k)


def _band_valid(q_pos, k_pos):
    qc = q_pos[:, None] // CHUNK
    kc = k_pos[None, :] // CHUNK
    return (k_pos[None, :] >= 0) & (kc <= qc) & (kc >= qc - N_PREV_CHUNKS)


def kernel(x_prompt, x_sample, cache_k, cache_v, state_conv, norm_ffn1, w_ffn1_gu, w_ffn1_down,
           norm_mix, w_in, b_gate, w_conv, rel_bias, w_conv_out, w_attn_out, w_o, norm_ffn2,
           w_ffn2_gu, w_ffn2_down, norm_final):
    depth = w_in.shape[0]
    b, s, d = x_prompt.shape
    nb, ts, _ = x_sample.shape
    r_cache = cache_k.shape[2]
    assert min(WINDOW, s) == WINDOW

    xp = x_prompt.reshape(b * s, d)
    xs = x_sample.reshape(nb * ts, d)
    w1gu, w1d = w_ffn1_gu.astype(BF16), w_ffn1_down.astype(BF16)
    w2gu, w2d = w_ffn2_gu.astype(BF16), w_ffn2_down.astype(BF16)
    win, wco, wao, wo = (w_in.astype(BF16), w_conv_out.astype(BF16), w_attn_out.astype(BF16),
                         w_o.astype(BF16))
    ck = cache_k.reshape(depth, nb, r_cache, ATTN_W).astype(BF16)
    cv = cache_v.reshape(depth, nb, r_cache, ATTN_W).astype(BF16)
    gfin = norm_final.reshape(1, d)

    qs_pos = PAST_LEN + np.arange(ts)
    ks_pos = np.concatenate([PAST_LEN - r_cache + np.arange(r_cache), qs_pos])
    all_valid_s = bool(_band_valid(qs_pos, ks_pos).all())

    outs = {n: [] for n in ("kp", "vp", "cp", "ks", "vs", "cs")}
    for l in range(depth):
        last = l == depth - 1
        g1 = norm_ffn1[l].reshape(1, d)
        gm = norm_mix[l].reshape(1, d)
        g2 = norm_ffn2[l].reshape(1, d)
        bg = b_gate[l].reshape(1, 2 * d)
        bias_p = _band_bias(rel_bias[l], WINDOW, CHUNK, BAND)
        bias_s = _band_bias(rel_bias[l], r_cache, ts, r_cache + ts)

        xp = _ffn(xp, g1, w1gu[l], w1d[l], gfin, final_norm=False, name=f"ffn1_prompt_l{l}")
        xs = _ffn(xs, g1, w1gu[l], w1d[l], gfin, final_norm=False, name=f"ffn1_sample_l{l}")

        xp3, kp, vp, cp = _mixer_prompt(xp.reshape(b, s, d), gm, win[l], bg, w_conv[l], bias_p,
                                        wco[l], wao[l], wo[l], name=f"mixer_prompt_l{l}")
        xp = xp3.reshape(b * s, d)
        xs, ks, vs, cs = _mixer_sample(xs, gm, win[l], bg, w_conv[l], bias_s, wco[l], wao[l],
                                       wo[l], ck[l], cv[l], state_conv[l], all_valid_s,
                                       name=f"mixer_sample_l{l}")

        xp = _ffn(xp, g2, w2gu[l], w2d[l], gfin, final_norm=last, name=f"ffn2_prompt_l{l}")
        xs = _ffn(xs, g2, w2gu[l], w2d[l], gfin, final_norm=last, name=f"ffn2_sample_l{l}")

        outs["kp"].append(kp.reshape(b, WINDOW, N_HEADS, HEAD_DIM))
        outs["vp"].append(vp.reshape(b, WINDOW, N_HEADS, HEAD_DIM))
        outs["cp"].append(cp)
        outs["ks"].append(ks.reshape(nb, ts, N_HEADS, HEAD_DIM))
        outs["vs"].append(vs.reshape(nb, ts, N_HEADS, HEAD_DIM))
        outs["cs"].append(cs)

    stack = lambda name: jnp.stack(outs[name], axis=0)
    return (xp.reshape(b, s, d), xs.reshape(nb, ts, d), stack("kp"), stack("vp"), stack("cp"),
            stack("ks"), stack("vs"), stack("cs"))
```

```python
import functools

import numpy as np
import jax
import jax.numpy as jnp
from jax import lax
from jax.experimental import pallas as pl
from jax.experimental.pallas import tpu as pltpu

CHUNK = 64
N_PREV_CHUNKS = 8
WINDOW = N_PREV_CHUNKS * CHUNK
BAND = WINDOW + CHUNK
N_HEADS = 8
HEAD_DIM = 64
ATTN_W = N_HEADS * HEAD_DIM
HEAD_PAIRS = N_HEADS // 2
PAIR_W = 2 * HEAD_DIM
V_EXT_W = 2 * PAIR_W
REL_CLIP = 128
CONV_K = 3
PAST_LEN = 2048
EPS = 1e-6
NEG_INF = -1e30

F32 = jnp.float32
BF16 = jnp.bfloat16

V7X_SCOPED_VMEM_MAX_BYTES = 60000 * 1024
V7X_MXU_DIM = 256
F32_SUBLANES = 8

FFN_TOKEN_TILE = 512
FFN_CHUNK_MAX = 1024


def _resident(shape):
    zeros = (0,) * len(shape)
    return pl.BlockSpec(shape, lambda *_: zeros, pipeline_mode=pl.Buffered(1))


def _layer_resident(stacked, layer):
    tail = (0,) * (stacked.ndim - 1)
    return pl.BlockSpec((None,) + stacked.shape[1:], lambda *_: (layer,) + tail,
                        pipeline_mode=pl.Buffered(1))


def _vmem_limit(estimate_bytes):
    return int(min(V7X_SCOPED_VMEM_MAX_BYTES, estimate_bytes))


def _rmsnorm(x, gain):
    return x * lax.rsqrt(jnp.mean(x * x, axis=-1, keepdims=True) + EPS) * gain


def _ff_chunks(d_ff):
    chunks = []
    left = d_ff
    while left > 0:
        ck = min(FFN_CHUNK_MAX, left)
        chunks.append(ck)
        left -= ck
    assert all(c % V7X_MXU_DIM == 0 for c in chunks), chunks
    return tuple(chunks)


def _ffn_body(x_ref, gain_ref, wgu_ref, wd_ref, gfin_ref, o_ref, *, chunks, d_ff, final_norm):
    x = x_ref[...]
    h = _rmsnorm(x, gain_ref[...]).astype(BF16)
    acc = None
    c0 = 0
    for ck in chunks:
        g = jnp.dot(h, wgu_ref[:, c0:c0 + ck], preferred_element_type=F32)
        u = jnp.dot(h, wgu_ref[:, d_ff + c0:d_ff + c0 + ck], preferred_element_type=F32)
        a = (jax.nn.silu(g) * u).astype(BF16)
        d = jnp.dot(a, wd_ref[c0:c0 + ck, :], preferred_element_type=F32)
        acc = d if acc is None else acc + d
        c0 += ck
    y = x + 0.5 * acc
    if final_norm:
        y = _rmsnorm(y, gfin_ref[...])
    o_ref[...] = y


def _ffn(x2d, gain, w_gu, w_down, gain_final, *, layer, final_norm, name):
    n, d = x2d.shape
    d_ff = w_down.shape[1]
    tm = min(FFN_TOKEN_TILE, n)
    assert n % tm == 0 and tm % F32_SUBLANES == 0
    chunks = _ff_chunks(d_ff)
    ck = max(chunks)
    est = (2 * (w_gu[0].size + w_down[0].size)
           + 4 * tm * d * 4
           + tm * (2 * ck * 4 + ck * 2 + d * 2 + 3 * d * 4)
           + (8 << 20))
    body = functools.partial(_ffn_body, chunks=chunks, d_ff=d_ff, final_norm=final_norm)
    return pl.pallas_call(
        body,
        grid=(n // tm,),
        in_specs=[
            pl.BlockSpec((tm, d), lambda i: (i, 0)),
            _resident((1, d)),
            _layer_resident(w_gu, layer),
            _layer_resident(w_down, layer),
            _resident((1, d)),
        ],
        out_specs=pl.BlockSpec((tm, d), lambda i: (i, 0)),
        out_shape=jax.ShapeDtypeStruct((n, d), F32),
        compiler_params=pltpu.CompilerParams(
            dimension_semantics=("arbitrary",),
            vmem_limit_bytes=_vmem_limit(est)),
        name=name,
    )(x2d, gain, w_gu, w_down, gain_final)


def _pair_scores(q_lo, q_hi, k_band):
    qs = jnp.concatenate([q_lo, q_hi], axis=0)
    return lax.dot_general(qs, k_band, (((1,), (1,)), ((), ())), preferred_element_type=F32)


def _pair_softmax_pv(s, v_band, bias, valid):
    rows = s.shape[0] // 2
    s = s + bias
    if valid is not None:
        s = jnp.where(valid, s, NEG_INF)
    m = jnp.max(s, axis=-1, keepdims=True)
    e = jnp.exp(s - m)
    o = jnp.dot(e.astype(BF16), v_band, preferred_element_type=F32)
    if v_band.shape[1] == V_EXT_W:
        o = o[:, :PAIR_W] * (1.0 / o[:, PAIR_W:])
    else:
        o = o * (1.0 / jnp.sum(e, axis=-1, keepdims=True))
    lane = lax.broadcasted_iota(jnp.int32, (rows, PAIR_W), 1)
    return jnp.where(lane < HEAD_DIM, o[:rows], o[rows:])


def _split_pair_queries(q):
    lane = lax.broadcasted_iota(jnp.int32, q.shape, 1)
    even = (lane % PAIR_W) < HEAD_DIM
    zero = jnp.zeros_like(q)
    return jnp.where(even, q, zero).astype(BF16), jnp.where(even, zero, q).astype(BF16)


def _short_conv(ubuf_ref, u, w_conv_ref, rows):
    ubuf_ref[pl.ds(F32_SUBLANES, rows), :] = u
    y = w_conv_ref[0:1, :] * ubuf_ref[pl.ds(F32_SUBLANES - 2, rows), :]
    y = y + w_conv_ref[1:2, :] * ubuf_ref[pl.ds(F32_SUBLANES - 1, rows), :]
    return y + w_conv_ref[2:3, :] * u


def _mixer_prompt_body(x_ref, gain_ref, win_ref, bg_ref, wconv_ref, bias_ref, wco_ref, wao_ref,
                       wo_ref, xo_ref, ko_ref, vo_ref, co_ref,
                       kbuf, vbuf, qlo, qhi, yattn, ubuf, sscr, *, conv_w):
    tm = WINDOW
    d = x_ref.shape[-1]
    t = pl.program_id(1)

    @pl.when(t == 0)
    def _():
        kbuf[0:tm, :] = jnp.zeros((tm, ATTN_W), BF16)
        vlane = lax.broadcasted_iota(jnp.int32, vbuf.shape, 1)
        vbuf[...] = jnp.where((vlane % V_EXT_W) < PAIR_W, 0.0, 1.0).astype(BF16)
        ubuf[0:F32_SUBLANES, :] = jnp.zeros((F32_SUBLANES, conv_w), F32)

    x = x_ref[...]
    h = _rmsnorm(x, gain_ref[...]).astype(BF16)

    zc = jnp.dot(h, win_ref[:, 0:3 * conv_w], preferred_element_type=F32)
    u = zc[:, conv_w:2 * conv_w] * zc[:, 2 * conv_w:3 * conv_w]
    y = _short_conv(ubuf, u, wconv_ref, tm)
    yconv = (zc[:, 0:conv_w] * y).astype(BF16)
    tail = u[tm - (CONV_K - 1):tm, :]
    co_ref[...] = tail
    ubuf[F32_SUBLANES - (CONV_K - 1):F32_SUBLANES, :] = tail

    q0 = 3 * conv_w
    zq = jnp.dot(h, win_ref[:, q0:q0 + 3 * ATTN_W], preferred_element_type=F32)
    k = zq[:, ATTN_W:2 * ATTN_W]
    v = zq[:, 2 * ATTN_W:3 * ATTN_W]
    ko_ref[...] = k
    vo_ref[...] = v
    kbuf[tm:2 * tm, :] = k.astype(BF16)
    v_bf = v.astype(BF16)
    for j in range(HEAD_PAIRS):
        vbuf[tm:2 * tm, j * V_EXT_W:j * V_EXT_W + PAIR_W] = v_bf[:, j * PAIR_W:(j + 1) * PAIR_W]
    q_even, q_odd = _split_pair_queries(zq[:, 0:ATTN_W] * (HEAD_DIM ** -0.5))
    qlo[...] = q_even
    qhi[...] = q_odd

    first_valid_row = jnp.where(t > 0, 0, WINDOW)
    col = lax.broadcasted_iota(jnp.int32, (2 * CHUNK, BAND), 1)

    def scores(item):
        c, j = item
        rows, lanes = slice(c * CHUNK, (c + 1) * CHUNK), slice(j * PAIR_W, (j + 1) * PAIR_W)
        return _pair_scores(qlo[rows, lanes], qhi[rows, lanes],
                            kbuf[c * CHUNK:c * CHUNK + BAND, lanes])

    items = [(c, j) for c in range(tm // CHUNK) for j in range(HEAD_PAIRS)]
    sscr[0] = scores(items[0])
    for i, (c, j) in enumerate(items):
        if i + 1 < len(items):
            sscr[(i + 1) % 2] = scores(items[i + 1])
        r0 = c * CHUNK
        out = _pair_softmax_pv(sscr[i % 2], vbuf[r0:r0 + BAND, j * V_EXT_W:(j + 1) * V_EXT_W],
                               bias_ref[j], (col + r0) >= first_valid_row)
        yattn[r0:r0 + CHUNK, j * PAIR_W:(j + 1) * PAIR_W] = out.astype(BF16)
    kbuf[0:tm, :] = kbuf[tm:2 * tm, :]
    vbuf[0:tm, :] = vbuf[tm:2 * tm, :]

    g0 = q0 + 3 * ATTN_W
    gate = jax.nn.sigmoid(jnp.dot(h, win_ref[:, g0:g0 + 2 * d], preferred_element_type=F32)
                          + bg_ref[...])
    mc = jnp.dot(yconv, wco_ref[...], preferred_element_type=F32)
    ma = jnp.dot(yattn[...], wao_ref[...], preferred_element_type=F32)
    m = gate[:, 0:d] * mc + gate[:, d:2 * d] * ma
    xo_ref[...] = x + jnp.dot(m.astype(BF16), wo_ref[...], preferred_element_type=F32)


def _mixer_prompt(x, gain, w_in, b_gate, w_conv, bias, w_co, w_ao, w_o, *, layer, name):
    b, s, d = x.shape
    conv_w = w_conv.shape[-1]
    tm = WINDOW
    assert s % tm == 0 and conv_w % 128 == 0
    weights = 2 * (w_in[0].size + w_co[0].size + w_ao[0].size + w_o[0].size) + 4 * bias.size
    est = (weights
           + 4 * tm * d * 4
           + 4 * tm * ATTN_W * 4
           + (2 * tm + 2 * 2 * tm + 3 * tm) * ATTN_W * 2
           + 2 * 2 * CHUNK * BAND * 4
           + (tm + F32_SUBLANES) * conv_w * 4
           + tm * (3 * conv_w + 3 * ATTN_W + 4 * d + 3 * d) * 4
           + (8 << 20))
    row = lambda bi, ti: (bi, ti, 0)
    per_batch = lambda bi, ti: (bi, 0, 0)
    body = functools.partial(_mixer_prompt_body, conv_w=conv_w)
    return pl.pallas_call(
        body,
        grid=(b, s // tm),
        in_specs=[
            pl.BlockSpec((None, tm, d), row),
            _resident((1, d)),
            _layer_resident(w_in, layer),
            _resident(b_gate.shape),
            _resident(w_conv.shape),
            _resident(bias.shape),
            _layer_resident(w_co, layer),
            _layer_resident(w_ao, layer),
            _layer_resident(w_o, layer),
        ],
        out_specs=[
            pl.BlockSpec((None, tm, d), row),
            pl.BlockSpec((None, tm, ATTN_W), per_batch),
            pl.BlockSpec((None, tm, ATTN_W), per_batch),
            pl.BlockSpec((None, CONV_K - 1, conv_w), per_batch),
        ],
        out_shape=[
            jax.ShapeDtypeStruct((b, s, d), F32),
            jax.ShapeDtypeStruct((b, tm, ATTN_W), F32),
            jax.ShapeDtypeStruct((b, tm, ATTN_W), F32),
            jax.ShapeDtypeStruct((b, CONV_K - 1, conv_w), F32),
        ],
        scratch_shapes=[
            pltpu.VMEM((2 * tm, ATTN_W), BF16),
            pltpu.VMEM((2 * tm, HEAD_PAIRS * V_EXT_W), BF16),
            pltpu.VMEM((tm, ATTN_W), BF16),
            pltpu.VMEM((tm, ATTN_W), BF16),
            pltpu.VMEM((tm, ATTN_W), BF16),
            pltpu.VMEM((tm + F32_SUBLANES, conv_w), F32),
            pltpu.VMEM((2, 2 * CHUNK, BAND), F32),
        ],
        compiler_params=pltpu.CompilerParams(
            dimension_semantics=("arbitrary", "arbitrary"),
            vmem_limit_bytes=_vmem_limit(est)),
        name=name,
    )(x, gain, w_in, b_gate, w_conv, bias, w_co, w_ao, w_o)


def _mixer_sample_body(x_ref, gain_ref, win_ref, bg_ref, wconv_ref, bias_ref, wco_ref, wao_ref,
                       wo_ref, ck_ref, cv_ref, st_ref, xo_ref, ko_ref, vo_ref, co_ref,
                       kband, vband, yconv, yattn, ubuf, *, conv_w, n_streams, t_new, all_valid):
    d = x_ref.shape[-1]
    r_cache = ck_ref.shape[1]
    x = x_ref[...]
    h = _rmsnorm(x, gain_ref[...]).astype(BF16)

    zc = jnp.dot(h, win_ref[:, 0:3 * conv_w], preferred_element_type=F32)
    u = zc[:, conv_w:2 * conv_w] * zc[:, 2 * conv_w:3 * conv_w]
    cb = zc[:, 0:conv_w]
    q0 = 3 * conv_w
    zq = jnp.dot(h, win_ref[:, q0:q0 + 3 * ATTN_W], preferred_element_type=F32)
    k = zq[:, ATTN_W:2 * ATTN_W]
    v = zq[:, 2 * ATTN_W:3 * ATTN_W]
    ko_ref[...] = k
    vo_ref[...] = v
    q_even, q_odd = _split_pair_queries(zq[:, 0:ATTN_W] * (HEAD_DIM ** -0.5))
    valid = None
    if not all_valid:
        shape = (2 * t_new, r_cache + t_new)
        q_chunk = (PAST_LEN + lax.broadcasted_iota(jnp.int32, shape, 0) % t_new) // CHUNK
        k_pos = PAST_LEN - r_cache + lax.broadcasted_iota(jnp.int32, shape, 1)
        k_chunk = lax.shift_right_arithmetic(k_pos, CHUNK.bit_length() - 1)
        valid = (k_pos >= 0) & (k_chunk <= q_chunk) & (k_chunk >= q_chunk - N_PREV_CHUNKS)

    for bi in range(n_streams):
        rows = slice(bi * t_new, (bi + 1) * t_new)
        ubuf[F32_SUBLANES - (CONV_K - 1):F32_SUBLANES, :] = st_ref[bi]
        y = _short_conv(ubuf, u[rows], wconv_ref, t_new)
        yconv[rows, :] = (cb[rows] * y).astype(BF16)
        co_ref[bi] = ubuf[pl.ds(F32_SUBLANES + t_new - (CONV_K - 1), CONV_K - 1), :]

        kband[0:r_cache, :] = ck_ref[bi]
        vband[0:r_cache, :] = cv_ref[bi]
        kband[r_cache:r_cache + t_new, :] = k[rows].astype(BF16)
        vband[r_cache:r_cache + t_new, :] = v[rows].astype(BF16)
        for j in range(HEAD_PAIRS):
            lanes = slice(j * PAIR_W, (j + 1) * PAIR_W)
            s = _pair_scores(q_even[rows, lanes], q_odd[rows, lanes], kband[:, lanes])
            out = _pair_softmax_pv(s, vband[:, lanes], bias_ref[j], valid)
            yattn[rows, lanes] = out.astype(BF16)

    g0 = q0 + 3 * ATTN_W
    gate = jax.nn.sigmoid(jnp.dot(h, win_ref[:, g0:g0 + 2 * d], preferred_element_type=F32)
                          + bg_ref[...])
    mc = jnp.dot(yconv[...], wco_ref[...], preferred_element_type=F32)
    ma = jnp.dot(yattn[...], wao_ref[...], preferred_element_type=F32)
    m = gate[:, 0:d] * mc + gate[:, d:2 * d] * ma
    xo_ref[...] = x + jnp.dot(m.astype(BF16), wo_ref[...], preferred_element_type=F32)


def _mixer_sample(x2d, gain, w_in, b_gate, w_conv, bias, w_co, w_ao, w_o, cache_k, cache_v,
                  state, all_valid, *, layer, name):
    n, d = x2d.shape
    _, n_streams, r_cache, _ = cache_k.shape
    t_new = n // n_streams
    conv_w = w_conv.shape[-1]
    n_keys = r_cache + t_new
    assert t_new % 16 == 0 and r_cache % 16 == 0
    stacked = (w_in, w_co, w_ao, w_o, cache_k, cache_v)
    operands = (x2d, gain, w_in, b_gate, w_conv, bias, w_co, w_ao, w_o, cache_k, cache_v, state)
    est = (sum(a[0].size * a.dtype.itemsize if any(a is w for w in stacked)
               else a.size * a.dtype.itemsize for a in operands)
           + 2 * (n * d + 2 * n * ATTN_W + n_streams * (CONV_K - 1) * conv_w) * 4
           + (2 * n_keys + 2 * n) * ATTN_W * 2
           + n * (3 * conv_w + 3 * ATTN_W + 4 * d + 3 * d) * 4
           + (8 << 20))
    body = functools.partial(_mixer_sample_body, conv_w=conv_w, n_streams=n_streams,
                             t_new=t_new, all_valid=all_valid)
    full = lambda shape: pl.BlockSpec(shape, lambda i: (0,) * len(shape))
    return pl.pallas_call(
        body,
        grid=(1,),
        in_specs=[_layer_resident(a, layer) if any(a is w for w in stacked) else _resident(a.shape)
                  for a in operands],
        out_specs=[
            full((n, d)),
            full((n, ATTN_W)),
            full((n, ATTN_W)),
            full((n_streams, CONV_K - 1, conv_w)),
        ],
        out_shape=[
            jax.ShapeDtypeStruct((n, d), F32),
            jax.ShapeDtypeStruct((n, ATTN_W), F32),
            jax.ShapeDtypeStruct((n, ATTN_W), F32),
            jax.ShapeDtypeStruct((n_streams, CONV_K - 1, conv_w), F32),
        ],
        scratch_shapes=[
            pltpu.VMEM((n_keys, ATTN_W), BF16),
            pltpu.VMEM((n_keys, ATTN_W), BF16),
            pltpu.VMEM((n, conv_w), BF16),
            pltpu.VMEM((n, ATTN_W), BF16),
            pltpu.VMEM((t_new + F32_SUBLANES, conv_w), F32),
        ],
        compiler_params=pltpu.CompilerParams(
            dimension_semantics=("arbitrary",),
            vmem_limit_bytes=_vmem_limit(est)),
        name=name,
    )(*operands)


def _band_bias(rel_bias_l, offset, n_q, n_k):
    period = n_k + n_q
    m = np.arange(period)
    m = np.where(m < n_k, m, m - period)
    idx = np.clip(offset - m, -REL_CLIP, REL_CLIP) + REL_CLIP
    vec = rel_bias_l[:, idx]
    h = vec.shape[0]
    flat = jnp.tile(vec, (1, n_q))[:, :n_q * (period - 1)]
    tiles = flat.reshape(h, n_q, period - 1)[:, :, :n_k]
    return tiles.reshape(h // 2, 2 * n_q, n_k)


def _band_valid(q_pos, k_pos):
    qc = q_pos[:, None] // CHUNK
    kc = k_pos[None, :] // CHUNK
    return (k_pos[None, :] >= 0) & (kc <= qc) & (kc >= qc - N_PREV_CHUNKS)


def kernel(x_prompt, x_sample, cache_k, cache_v, state_conv, norm_ffn1, w_ffn1_gu, w_ffn1_down,
           norm_mix, w_in, b_gate, w_conv, rel_bias, w_conv_out, w_attn_out, w_o, norm_ffn2,
           w_ffn2_gu, w_ffn2_down, norm_final):
    depth = w_in.shape[0]
    b, s, d = x_prompt.shape
    nb, ts, _ = x_sample.shape
    r_cache = cache_k.shape[2]
    assert min(WINDOW, s) == WINDOW

    xp = x_prompt.reshape(b * s, d)
    xs = x_sample.reshape(nb * ts, d)
    w1gu, w1d = w_ffn1_gu.astype(BF16), w_ffn1_down.astype(BF16)
    w2gu, w2d = w_ffn2_gu.astype(BF16), w_ffn2_down.astype(BF16)
    win, wco, wao, wo = (w_in.astype(BF16), w_conv_out.astype(BF16), w_attn_out.astype(BF16),
                         w_o.astype(BF16))
    ck = cache_k.reshape(depth, nb, r_cache, ATTN_W).astype(BF16)
    cv = cache_v.reshape(depth, nb, r_cache, ATTN_W).astype(BF16)
    gfin = norm_final.reshape(1, d)

    qs_pos = PAST_LEN + np.arange(ts)
    ks_pos = np.concatenate([PAST_LEN - r_cache + np.arange(r_cache), qs_pos])
    all_valid_s = bool(_band_valid(qs_pos, ks_pos).all())

    outs = {n: [] for n in ("kp", "vp", "cp", "ks", "vs", "cs")}
    for l in range(depth):
        last = l == depth - 1
        g1 = norm_ffn1[l].reshape(1, d)
        gm = norm_mix[l].reshape(1, d)
        g2 = norm_ffn2[l].reshape(1, d)
        bg = b_gate[l].reshape(1, 2 * d)
        bias_p = _band_bias(rel_bias[l], WINDOW, CHUNK, BAND)
        bias_s = _band_bias(rel_bias[l], r_cache, ts, r_cache + ts)

        xp = _ffn(xp, g1, w1gu, w1d, gfin, layer=l, final_norm=False, name=f"ffn1_prompt_l{l}")
        xs = _ffn(xs, g1, w1gu, w1d, gfin, layer=l, final_norm=False, name=f"ffn1_sample_l{l}")

        xp3, kp, vp, cp = _mixer_prompt(xp.reshape(b, s, d), gm, win, bg, w_conv[l], bias_p,
                                        wco, wao, wo, layer=l, name=f"mixer_prompt_l{l}")
        xp = xp3.reshape(b * s, d)
        xs, ks, vs, cs = _mixer_sample(xs, gm, win, bg, w_conv[l], bias_s, wco, wao, wo, ck, cv,
                                       state_conv[l], all_valid_s, layer=l,
                                       name=f"mixer_sample_l{l}")

        xp = _ffn(xp, g2, w2gu, w2d, gfin, layer=l, final_norm=last, name=f"ffn2_prompt_l{l}")
        xs = _ffn(xs, g2, w2gu, w2d, gfin, layer=l, final_norm=last, name=f"ffn2_sample_l{l}")

        outs["kp"].append(kp.reshape(b, WINDOW, N_HEADS, HEAD_DIM))
        outs["vp"].append(vp.reshape(b, WINDOW, N_HEADS, HEAD_DIM))
        outs["cp"].append(cp)
        outs["ks"].append(ks.reshape(nb, ts, N_HEADS, HEAD_DIM))
        outs["vs"].append(vs.reshape(nb, ts, N_HEADS, HEAD_DIM))
        outs["cs"].append(cs)

    stack = lambda name: jnp.stack(outs[name], axis=0)
    return (xp.reshape(b, s, d), xs.reshape(nb, ts, d), stack("kp"), stack("vp"), stack("cp"),
            stack("ks"), stack("vs"), stack("cs"))
```

```python
import functools

import numpy as np
import jax
import jax.numpy as jnp
from jax import lax
from jax.experimental import pallas as pl
from jax.experimental.pallas import tpu as pltpu

CHUNK = 64
N_PREV_CHUNKS = 8
WINDOW = N_PREV_CHUNKS * CHUNK
BAND = WINDOW + CHUNK
N_HEADS = 8
HEAD_DIM = 64
ATTN_W = N_HEADS * HEAD_DIM
HEAD_PAIRS = N_HEADS // 2
PAIR_W = 2 * HEAD_DIM
V_EXT_W = 2 * PAIR_W
REL_CLIP = 128
CONV_K = 3
PAST_LEN = 2048
EPS = 1e-6
NEG_INF = -1e30

F32 = jnp.float32
BF16 = jnp.bfloat16

V7X_SCOPED_VMEM_MAX_BYTES = 60000 * 1024
V7X_MXU_DIM = 256
F32_SUBLANES = 8

FFN_TOKEN_TILE = 1024
FFN_SUB_ROWS = 256
FFN_CHUNK_MAX = 1536


def _resident(shape):
    zeros = (0,) * len(shape)
    return pl.BlockSpec(shape, lambda *_: zeros, pipeline_mode=pl.Buffered(1))


def _layer_resident(stacked, layer):
    tail = (0,) * (stacked.ndim - 1)
    return pl.BlockSpec((None,) + stacked.shape[1:], lambda *_: (layer,) + tail,
                        pipeline_mode=pl.Buffered(1))


def _vmem_limit(estimate_bytes):
    return int(min(V7X_SCOPED_VMEM_MAX_BYTES, estimate_bytes))


def _rmsnorm(x, gain):
    return x * lax.rsqrt(jnp.mean(x * x, axis=-1, keepdims=True) + EPS) * gain


def _ff_chunks(d_ff):
    chunks = []
    left = d_ff
    while left > 0:
        ck = min(FFN_CHUNK_MAX, left)
        chunks.append(ck)
        left -= ck
    assert all(c % V7X_MXU_DIM == 0 for c in chunks), chunks
    return tuple(chunks)


def _ffn_body(x_ref, gain_ref, wgu_ref, wd_ref, gfin_ref, o_ref, *, sub_rows, chunks, d_ff,
              final_norm):
    for r0 in range(0, x_ref.shape[0], sub_rows):
        x = x_ref[r0:r0 + sub_rows, :]
        h = _rmsnorm(x, gain_ref[...]).astype(BF16)
        acc = None
        c0 = 0
        for ck in chunks:
            g = jnp.dot(h, wgu_ref[:, c0:c0 + ck], preferred_element_type=F32)
            u = jnp.dot(h, wgu_ref[:, d_ff + c0:d_ff + c0 + ck], preferred_element_type=F32)
            a = (jax.nn.silu(g) * u).astype(BF16)
            d = jnp.dot(a, wd_ref[c0:c0 + ck, :], preferred_element_type=F32)
            acc = d if acc is None else acc + d
            c0 += ck
        y = x + 0.5 * acc
        if final_norm:
            y = _rmsnorm(y, gfin_ref[...])
        o_ref[r0:r0 + sub_rows, :] = y


def _ffn(x2d, gain, w_gu, w_down, gain_final, *, layer, final_norm, name):
    n, d = x2d.shape
    d_ff = w_down.shape[1]
    tm = min(FFN_TOKEN_TILE, n)
    sub = min(FFN_SUB_ROWS, tm)
    assert n % tm == 0 and tm % sub == 0 and sub % F32_SUBLANES == 0
    chunks = _ff_chunks(d_ff)
    ck = max(chunks)
    est = (2 * (w_gu[0].size + w_down[0].size)
           + 4 * tm * d * 4
           + 2 * sub * (2 * ck * 4 + ck * 2 + d * 2 + 3 * d * 4)
           + (8 << 20))
    body = functools.partial(_ffn_body, sub_rows=sub, chunks=chunks, d_ff=d_ff,
                             final_norm=final_norm)
    return pl.pallas_call(
        body,
        grid=(n // tm,),
        in_specs=[
            pl.BlockSpec((tm, d), lambda i: (i, 0)),
            _resident((1, d)),
            _layer_resident(w_gu, layer),
            _layer_resident(w_down, layer),
            _resident((1, d)),
        ],
        out_specs=pl.BlockSpec((tm, d), lambda i: (i, 0)),
        out_shape=jax.ShapeDtypeStruct((n, d), F32),
        compiler_params=pltpu.CompilerParams(
            dimension_semantics=("arbitrary",),
            vmem_limit_bytes=_vmem_limit(est)),
        name=name,
    )(x2d, gain, w_gu, w_down, gain_final)


def _pair_scores(q_lo, q_hi, k_band):
    qs = jnp.concatenate([q_lo, q_hi], axis=0)
    return lax.dot_general(qs, k_band, (((1,), (1,)), ((), ())), preferred_element_type=F32)


def _pair_softmax_pv(s, v_band, bias, valid):
    rows = s.shape[0] // 2
    s = s + bias
    if valid is not None:
        s = jnp.where(valid, s, NEG_INF)
    m = jnp.max(s, axis=-1, keepdims=True)
    e = jnp.exp(s - m)
    o = jnp.dot(e.astype(BF16), v_band, preferred_element_type=F32)
    if v_band.shape[1] == V_EXT_W:
        o = o[:, :PAIR_W] * (1.0 / o[:, PAIR_W:])
    else:
        o = o * (1.0 / jnp.sum(e, axis=-1, keepdims=True))
    lane = lax.broadcasted_iota(jnp.int32, (rows, PAIR_W), 1)
    return jnp.where(lane < HEAD_DIM, o[:rows], o[rows:])


def _split_pair_queries(q):
    lane = lax.broadcasted_iota(jnp.int32, q.shape, 1)
    even = (lane % PAIR_W) < HEAD_DIM
    zero = jnp.zeros_like(q)
    return jnp.where(even, q, zero).astype(BF16), jnp.where(even, zero, q).astype(BF16)


def _short_conv(ubuf_ref, u, w_conv_ref, rows):
    ubuf_ref[pl.ds(F32_SUBLANES, rows), :] = u
    y = w_conv_ref[0:1, :] * ubuf_ref[pl.ds(F32_SUBLANES - 2, rows), :]
    y = y + w_conv_ref[1:2, :] * ubuf_ref[pl.ds(F32_SUBLANES - 1, rows), :]
    return y + w_conv_ref[2:3, :] * u


def _mixer_prompt_body(x_ref, gain_ref, win_ref, bg_ref, wconv_ref, bias_ref, wco_ref, wao_ref,
                       wo_ref, xo_ref, ko_ref, vo_ref, co_ref,
                       kbuf, vbuf, qlo, qhi, yattn, ubuf, sscr, *, conv_w):
    tm = WINDOW
    d = x_ref.shape[-1]
    t = pl.program_id(1)

    @pl.when(t == 0)
    def _():
        kbuf[0:tm, :] = jnp.zeros((tm, ATTN_W), BF16)
        vlane = lax.broadcasted_iota(jnp.int32, vbuf.shape, 1)
        vbuf[...] = jnp.where((vlane % V_EXT_W) < PAIR_W, 0.0, 1.0).astype(BF16)
        ubuf[0:F32_SUBLANES, :] = jnp.zeros((F32_SUBLANES, conv_w), F32)

    x = x_ref[...]
    h = _rmsnorm(x, gain_ref[...]).astype(BF16)

    zc = jnp.dot(h, win_ref[:, 0:3 * conv_w], preferred_element_type=F32)
    u = zc[:, conv_w:2 * conv_w] * zc[:, 2 * conv_w:3 * conv_w]
    y = _short_conv(ubuf, u, wconv_ref, tm)
    yconv = (zc[:, 0:conv_w] * y).astype(BF16)
    tail = u[tm - (CONV_K - 1):tm, :]
    co_ref[...] = tail
    ubuf[F32_SUBLANES - (CONV_K - 1):F32_SUBLANES, :] = tail

    q0 = 3 * conv_w
    zq = jnp.dot(h, win_ref[:, q0:q0 + 3 * ATTN_W], preferred_element_type=F32)
    k = zq[:, ATTN_W:2 * ATTN_W]
    v = zq[:, 2 * ATTN_W:3 * ATTN_W]
    ko_ref[...] = k
    vo_ref[...] = v
    kbuf[tm:2 * tm, :] = k.astype(BF16)
    v_bf = v.astype(BF16)
    for j in range(HEAD_PAIRS):
        vbuf[tm:2 * tm, j * V_EXT_W:j * V_EXT_W + PAIR_W] = v_bf[:, j * PAIR_W:(j + 1) * PAIR_W]
    q_even, q_odd = _split_pair_queries(zq[:, 0:ATTN_W] * (HEAD_DIM ** -0.5))
    qlo[...] = q_even
    qhi[...] = q_odd

    first_valid_row = jnp.where(t > 0, 0, WINDOW)
    col = lax.broadcasted_iota(jnp.int32, (2 * CHUNK, BAND), 1)

    def scores(item):
        c, j = item
        rows, lanes = slice(c * CHUNK, (c + 1) * CHUNK), slice(j * PAIR_W, (j + 1) * PAIR_W)
        return _pair_scores(qlo[rows, lanes], qhi[rows, lanes],
                            kbuf[c * CHUNK:c * CHUNK + BAND, lanes])

    items = [(c, j) for c in range(tm // CHUNK) for j in range(HEAD_PAIRS)]
    sscr[0] = scores(items[0])
    for i, (c, j) in enumerate(items):
        if i + 1 < len(items):
            sscr[(i + 1) % 2] = scores(items[i + 1])
        r0 = c * CHUNK
        out = _pair_softmax_pv(sscr[i % 2], vbuf[r0:r0 + BAND, j * V_EXT_W:(j + 1) * V_EXT_W],
                               bias_ref[j], (col + r0) >= first_valid_row)
        yattn[r0:r0 + CHUNK, j * PAIR_W:(j + 1) * PAIR_W] = out.astype(BF16)
    kbuf[0:tm, :] = kbuf[tm:2 * tm, :]
    vbuf[0:tm, :] = vbuf[tm:2 * tm, :]

    g0 = q0 + 3 * ATTN_W
    gate = jax.nn.sigmoid(jnp.dot(h, win_ref[:, g0:g0 + 2 * d], preferred_element_type=F32)
                          + bg_ref[...])
    mc = jnp.dot(yconv, wco_ref[...], preferred_element_type=F32)
    ma = jnp.dot(yattn[...], wao_ref[...], preferred_element_type=F32)
    m = gate[:, 0:d] * mc + gate[:, d:2 * d] * ma
    xo_ref[...] = x + jnp.dot(m.astype(BF16), wo_ref[...], preferred_element_type=F32)


def _mixer_prompt(x, gain, w_in, b_gate, w_conv, bias, w_co, w_ao, w_o, *, layer, name):
    b, s, d = x.shape
    conv_w = w_conv.shape[-1]
    tm = WINDOW
    assert s % tm == 0 and conv_w % 128 == 0
    weights = 2 * (w_in[0].size + w_co[0].size + w_ao[0].size + w_o[0].size) + 4 * bias.size
    est = (weights
           + 4 * tm * d * 4
           + 4 * tm * ATTN_W * 4
           + (2 * tm + 2 * 2 * tm + 3 * tm) * ATTN_W * 2
           + 2 * 2 * CHUNK * BAND * 4
           + (tm + F32_SUBLANES) * conv_w * 4
           + tm * (3 * conv_w + 3 * ATTN_W + 4 * d + 3 * d) * 4
           + (8 << 20))
    row = lambda bi, ti: (bi, ti, 0)
    per_batch = lambda bi, ti: (bi, 0, 0)
    body = functools.partial(_mixer_prompt_body, conv_w=conv_w)
    return pl.pallas_call(
        body,
        grid=(b, s // tm),
        in_specs=[
            pl.BlockSpec((None, tm, d), row),
            _resident((1, d)),
            _layer_resident(w_in, layer),
            _resident(b_gate.shape),
            _resident(w_conv.shape),
            _resident(bias.shape),
            _layer_resident(w_co, layer),
            _layer_resident(w_ao, layer),
            _layer_resident(w_o, layer),
        ],
        out_specs=[
            pl.BlockSpec((None, tm, d), row),
            pl.BlockSpec((None, tm, ATTN_W), per_batch),
            pl.BlockSpec((None, tm, ATTN_W), per_batch),
            pl.BlockSpec((None, CONV_K - 1, conv_w), per_batch),
        ],
        out_shape=[
            jax.ShapeDtypeStruct((b, s, d), F32),
            jax.ShapeDtypeStruct((b, tm, ATTN_W), F32),
            jax.ShapeDtypeStruct((b, tm, ATTN_W), F32),
            jax.ShapeDtypeStruct((b, CONV_K - 1, conv_w), F32),
        ],
        scratch_shapes=[
            pltpu.VMEM((2 * tm, ATTN_W), BF16),
            pltpu.VMEM((2 * tm, HEAD_PAIRS * V_EXT_W), BF16),
            pltpu.VMEM((tm, ATTN_W), BF16),
            pltpu.VMEM((tm, ATTN_W), BF16),
            pltpu.VMEM((tm, ATTN_W), BF16),
            pltpu.VMEM((tm + F32_SUBLANES, conv_w), F32),
            pltpu.VMEM((2, 2 * CHUNK, BAND), F32),
        ],
        compiler_params=pltpu.CompilerParams(
            dimension_semantics=("arbitrary", "arbitrary"),
            vmem_limit_bytes=_vmem_limit(est)),
        name=name,
    )(x, gain, w_in, b_gate, w_conv, bias, w_co, w_ao, w_o)


def _mixer_sample_body(x_ref, gain_ref, win_ref, bg_ref, wconv_ref, bias_ref, wco_ref, wao_ref,
                       wo_ref, ck_ref, cv_ref, st_ref, xo_ref, ko_ref, vo_ref, co_ref,
                       kband, vband, yconv, yattn, ubuf, *, conv_w, n_streams, t_new, all_valid):
    d = x_ref.shape[-1]
    r_cache = ck_ref.shape[1]
    x = x_ref[...]
    h = _rmsnorm(x, gain_ref[...]).astype(BF16)

    zc = jnp.dot(h, win_ref[:, 0:3 * conv_w], preferred_element_type=F32)
    u = zc[:, conv_w:2 * conv_w] * zc[:, 2 * conv_w:3 * conv_w]
    cb = zc[:, 0:conv_w]
    q0 = 3 * conv_w
    zq = jnp.dot(h, win_ref[:, q0:q0 + 3 * ATTN_W], preferred_element_type=F32)
    k = zq[:, ATTN_W:2 * ATTN_W]
    v = zq[:, 2 * ATTN_W:3 * ATTN_W]
    ko_ref[...] = k
    vo_ref[...] = v
    q_even, q_odd = _split_pair_queries(zq[:, 0:ATTN_W] * (HEAD_DIM ** -0.5))
    valid = None
    if not all_valid:
        shape = (2 * t_new, r_cache + t_new)
        q_chunk = (PAST_LEN + lax.broadcasted_iota(jnp.int32, shape, 0) % t_new) // CHUNK
        k_pos = PAST_LEN - r_cache + lax.broadcasted_iota(jnp.int32, shape, 1)
        k_chunk = lax.shift_right_arithmetic(k_pos, CHUNK.bit_length() - 1)
        valid = (k_pos >= 0) & (k_chunk <= q_chunk) & (k_chunk >= q_chunk - N_PREV_CHUNKS)

    for bi in range(n_streams):
        rows = slice(bi * t_new, (bi + 1) * t_new)
        ubuf[F32_SUBLANES - (CONV_K - 1):F32_SUBLANES, :] = st_ref[bi]
        y = _short_conv(ubuf, u[rows], wconv_ref, t_new)
        yconv[rows, :] = (cb[rows] * y).astype(BF16)
        co_ref[bi] = ubuf[pl.ds(F32_SUBLANES + t_new - (CONV_K - 1), CONV_K - 1), :]

        kband[0:r_cache, :] = ck_ref[bi]
        vband[0:r_cache, :] = cv_ref[bi]
        kband[r_cache:r_cache + t_new, :] = k[rows].astype(BF16)
        vband[r_cache:r_cache + t_new, :] = v[rows].astype(BF16)
        for j in range(HEAD_PAIRS):
            lanes = slice(j * PAIR_W, (j + 1) * PAIR_W)
            s = _pair_scores(q_even[rows, lanes], q_odd[rows, lanes], kband[:, lanes])
            out = _pair_softmax_pv(s, vband[:, lanes], bias_ref[j], valid)
            yattn[rows, lanes] = out.astype(BF16)

    g0 = q0 + 3 * ATTN_W
    gate = jax.nn.sigmoid(jnp.dot(h, win_ref[:, g0:g0 + 2 * d], preferred_element_type=F32)
                          + bg_ref[...])
    mc = jnp.dot(yconv[...], wco_ref[...], preferred_element_type=F32)
    ma = jnp.dot(yattn[...], wao_ref[...], preferred_element_type=F32)
    m = gate[:, 0:d] * mc + gate[:, d:2 * d] * ma
    xo_ref[...] = x + jnp.dot(m.astype(BF16), wo_ref[...], preferred_element_type=F32)


def _mixer_sample(x2d, gain, w_in, b_gate, w_conv, bias, w_co, w_ao, w_o, cache_k, cache_v,
                  state, all_valid, *, layer, name):
    n, d = x2d.shape
    _, n_streams, r_cache, _ = cache_k.shape
    t_new = n // n_streams
    conv_w = w_conv.shape[-1]
    n_keys = r_cache + t_new
    assert t_new % 16 == 0 and r_cache % 16 == 0
    stacked = (w_in, w_co, w_ao, w_o, cache_k, cache_v)
    operands = (x2d, gain, w_in, b_gate, w_conv, bias, w_co, w_ao, w_o, cache_k, cache_v, state)
    est = (sum(a[0].size * a.dtype.itemsize if any(a is w for w in stacked)
               else a.size * a.dtype.itemsize for a in operands)
           + 2 * (n * d + 2 * n * ATTN_W + n_streams * (CONV_K - 1) * conv_w) * 4
           + (2 * n_keys + 2 * n) * ATTN_W * 2
           + n * (3 * conv_w + 3 * ATTN_W + 4 * d + 3 * d) * 4
           + (8 << 20))
    body = functools.partial(_mixer_sample_body, conv_w=conv_w, n_streams=n_streams,
                             t_new=t_new, all_valid=all_valid)
    full = lambda shape: pl.BlockSpec(shape, lambda i: (0,) * len(shape))
    return pl.pallas_call(
        body,
        grid=(1,),
        in_specs=[_layer_resident(a, layer) if any(a is w for w in stacked) else _resident(a.shape)
                  for a in operands],
        out_specs=[
            full((n, d)),
            full((n, ATTN_W)),
            full((n, ATTN_W)),
            full((n_streams, CONV_K - 1, conv_w)),
        ],
        out_shape=[
            jax.ShapeDtypeStruct((n, d), F32),
            jax.ShapeDtypeStruct((n, ATTN_W), F32),
            jax.ShapeDtypeStruct((n, ATTN_W), F32),
            jax.ShapeDtypeStruct((n_streams, CONV_K - 1, conv_w), F32),
        ],
        scratch_shapes=[
            pltpu.VMEM((n_keys, ATTN_W), BF16),
            pltpu.VMEM((n_keys, ATTN_W), BF16),
            pltpu.VMEM((n, conv_w), BF16),
            pltpu.VMEM((n, ATTN_W), BF16),
            pltpu.VMEM((t_new + F32_SUBLANES, conv_w), F32),
        ],
        compiler_params=pltpu.CompilerParams(
            dimension_semantics=("arbitrary",),
            vmem_limit_bytes=_vmem_limit(est)),
        name=name,
    )(*operands)


def _band_bias(rel_bias_l, offset, n_q, n_k):
    period = n_k + n_q
    m = np.arange(period)
    m = np.where(m < n_k, m, m - period)
    idx = np.clip(offset - m, -REL_CLIP, REL_CLIP) + REL_CLIP
    vec = rel_bias_l[:, idx]
    h = vec.shape[0]
    flat = jnp.tile(vec, (1, n_q))[:, :n_q * (period - 1)]
    tiles = flat.reshape(h, n_q, period - 1)[:, :, :n_k]
    return tiles.reshape(h // 2, 2 * n_q, n_k)


def _band_valid(q_pos, k_pos):
    qc = q_pos[:, None] // CHUNK
    kc = k_pos[None, :] // CHUNK
    return (k_pos[None, :] >= 0) & (kc <= qc) & (kc >= qc - N_PREV_CHUNKS)


def kernel(x_prompt, x_sample, cache_k, cache_v, state_conv, norm_ffn1, w_ffn1_gu, w_ffn1_down,
           norm_mix, w_in, b_gate, w_conv, rel_bias, w_conv_out, w_attn_out, w_o, norm_ffn2,
           w_ffn2_gu, w_ffn2_down, norm_final):
    depth = w_in.shape[0]
    b, s, d = x_prompt.shape
    nb, ts, _ = x_sample.shape
    r_cache = cache_k.shape[2]
    assert min(WINDOW, s) == WINDOW

    xp = x_prompt.reshape(b * s, d)
    xs = x_sample.reshape(nb * ts, d)
    w1gu, w1d = w_ffn1_gu.astype(BF16), w_ffn1_down.astype(BF16)
    w2gu, w2d = w_ffn2_gu.astype(BF16), w_ffn2_down.astype(BF16)
    win, wco, wao, wo = (w_in.astype(BF16), w_conv_out.astype(BF16), w_attn_out.astype(BF16),
                         w_o.astype(BF16))
    ck = cache_k.reshape(depth, nb, r_cache, ATTN_W).astype(BF16)
    cv = cache_v.reshape(depth, nb, r_cache, ATTN_W).astype(BF16)
    gfin = norm_final.reshape(1, d)

    qs_pos = PAST_LEN + np.arange(ts)
    ks_pos = np.concatenate([PAST_LEN - r_cache + np.arange(r_cache), qs_pos])
    all_valid_s = bool(_band_valid(qs_pos, ks_pos).all())

    outs = {n: [] for n in ("kp", "vp", "cp", "ks", "vs", "cs")}
    for l in range(depth):
        last = l == depth - 1
        g1 = norm_ffn1[l].reshape(1, d)
        gm = norm_mix[l].reshape(1, d)
        g2 = norm_ffn2[l].reshape(1, d)
        bg = b_gate[l].reshape(1, 2 * d)
        bias_p = _band_bias(rel_bias[l], WINDOW, CHUNK, BAND)
        bias_s = _band_bias(rel_bias[l], r_cache, ts, r_cache + ts)

        xp = _ffn(xp, g1, w1gu, w1d, gfin, layer=l, final_norm=False, name=f"ffn1_prompt_l{l}")
        xs = _ffn(xs, g1, w1gu, w1d, gfin, layer=l, final_norm=False, name=f"ffn1_sample_l{l}")

        xp3, kp, vp, cp = _mixer_prompt(xp.reshape(b, s, d), gm, win, bg, w_conv[l], bias_p,
                                        wco, wao, wo, layer=l, name=f"mixer_prompt_l{l}")
        xp = xp3.reshape(b * s, d)
        xs, ks, vs, cs = _mixer_sample(xs, gm, win, bg, w_conv[l], bias_s, wco, wao, wo, ck, cv,
                                       state_conv[l], all_valid_s, layer=l,
                                       name=f"mixer_sample_l{l}")

        xp = _ffn(xp, g2, w2gu, w2d, gfin, layer=l, final_norm=last, name=f"ffn2_prompt_l{l}")
        xs = _ffn(xs, g2, w2gu, w2d, gfin, layer=l, final_norm=last, name=f"ffn2_sample_l{l}")

        outs["kp"].append(kp.reshape(b, WINDOW, N_HEADS, HEAD_DIM))
        outs["vp"].append(vp.reshape(b, WINDOW, N_HEADS, HEAD_DIM))
        outs["cp"].append(cp)
        outs["ks"].append(ks.reshape(nb, ts, N_HEADS, HEAD_DIM))
        outs["vs"].append(vs.reshape(nb, ts, N_HEADS, HEAD_DIM))
        outs["cs"].append(cs)

    stack = lambda name: jnp.stack(outs[name], axis=0)
    return (xp.reshape(b, s, d), xs.reshape(nb, ts, d), stack("kp"), stack("vp"), stack("cp"),
            stack("ks"), stack("vs"), stack("cs"))
```

```python
import functools

import numpy as np
import jax
import jax.numpy as jnp
from jax import lax
from jax.experimental import pallas as pl
from jax.experimental.pallas import tpu as pltpu

CHUNK = 64
N_PREV_CHUNKS = 8
WINDOW = N_PREV_CHUNKS * CHUNK
BAND = WINDOW + CHUNK
N_HEADS = 8
HEAD_DIM = 64
ATTN_W = N_HEADS * HEAD_DIM
HEAD_PAIRS = N_HEADS // 2
PAIR_W = 2 * HEAD_DIM
V_EXT_W = 2 * PAIR_W
REL_CLIP = 128
CONV_K = 3
PAST_LEN = 2048
EPS = 1e-6
NEG_INF = -1e30

F32 = jnp.float32
BF16 = jnp.bfloat16

V7X_SCOPED_VMEM_MAX_BYTES = 60000 * 1024
V7X_MXU_DIM = 256
F32_SUBLANES = 8

MIX_WINDOWS_PER_STEP = 2
FFN_TOKEN_TILE = 1024
FFN_SUB_ROWS = 256
FFN_CHUNK_MAX = 1536


def _resident(shape):
    zeros = (0,) * len(shape)
    return pl.BlockSpec(shape, lambda *_: zeros, pipeline_mode=pl.Buffered(1))


def _layer_resident(stacked, layer):
    tail = (0,) * (stacked.ndim - 1)
    return pl.BlockSpec((None,) + stacked.shape[1:], lambda *_: (layer,) + tail,
                        pipeline_mode=pl.Buffered(1))


def _vmem_limit(estimate_bytes):
    return int(min(V7X_SCOPED_VMEM_MAX_BYTES, estimate_bytes))


def _rmsnorm(x, gain):
    return x * lax.rsqrt(jnp.mean(x * x, axis=-1, keepdims=True) + EPS) * gain


def _ff_chunks(d_ff):
    chunks = []
    left = d_ff
    while left > 0:
        ck = min(FFN_CHUNK_MAX, left)
        chunks.append(ck)
        left -= ck
    assert all(c % V7X_MXU_DIM == 0 for c in chunks), chunks
    return tuple(chunks)


def _ffn_rows(x, gain_ref, wgu_ref, wd_ref, gfin_ref, *, chunks, d_ff, final_norm):
    h = _rmsnorm(x, gain_ref[...]).astype(BF16)
    acc = None
    c0 = 0
    for ck in chunks:
        g = jnp.dot(h, wgu_ref[:, c0:c0 + ck], preferred_element_type=F32)
        u = jnp.dot(h, wgu_ref[:, d_ff + c0:d_ff + c0 + ck], preferred_element_type=F32)
        a = (jax.nn.silu(g) * u).astype(BF16)
        d = jnp.dot(a, wd_ref[c0:c0 + ck, :], preferred_element_type=F32)
        acc = d if acc is None else acc + d
        c0 += ck
    y = x + 0.5 * acc
    if final_norm:
        y = _rmsnorm(y, gfin_ref[...])
    return y


def _ffn_body(x_ref, xs_ref, gain_ref, wgu_ref, wd_ref, gfin_ref, o_ref, os_ref, *, sub_rows,
              **static):
    weights = (gain_ref, wgu_ref, wd_ref, gfin_ref)
    for r0 in range(0, x_ref.shape[0], sub_rows):
        o_ref[r0:r0 + sub_rows, :] = _ffn_rows(x_ref[r0:r0 + sub_rows, :], *weights, **static)

    @pl.when(pl.program_id(0) == pl.num_programs(0) - 1)
    def _():
        os_ref[...] = _ffn_rows(xs_ref[...], *weights, **static)


def _ffn(x2d, xs2d, gain, w_gu, w_down, gain_final, *, layer, final_norm, name):
    n, d = x2d.shape
    ns = xs2d.shape[0]
    d_ff = w_down.shape[1]
    tm = min(FFN_TOKEN_TILE, n)
    sub = min(FFN_SUB_ROWS, tm)
    assert n % tm == 0 and tm % sub == 0 and sub % F32_SUBLANES == 0 and ns % F32_SUBLANES == 0
    chunks = _ff_chunks(d_ff)
    ck = max(chunks)
    est = (2 * (w_gu[0].size + w_down[0].size)
           + 4 * tm * d * 4 + 3 * ns * d * 4
           + 2 * sub * (2 * ck * 4 + ck * 2 + d * 2 + 3 * d * 4)
           + (8 << 20))
    body = functools.partial(_ffn_body, sub_rows=sub, chunks=chunks, d_ff=d_ff,
                             final_norm=final_norm)
    return pl.pallas_call(
        body,
        grid=(n // tm,),
        in_specs=[
            pl.BlockSpec((tm, d), lambda i: (i, 0)),
            _resident((ns, d)),
            _resident((1, d)),
            _layer_resident(w_gu, layer),
            _layer_resident(w_down, layer),
            _resident((1, d)),
        ],
        out_specs=[pl.BlockSpec((tm, d), lambda i: (i, 0)),
                   pl.BlockSpec((ns, d), lambda i: (0, 0))],
        out_shape=[jax.ShapeDtypeStruct((n, d), F32), jax.ShapeDtypeStruct((ns, d), F32)],
        compiler_params=pltpu.CompilerParams(
            dimension_semantics=("arbitrary",),
            vmem_limit_bytes=_vmem_limit(est)),
        name=name,
    )(x2d, xs2d, gain, w_gu, w_down, gain_final)


def _pair_scores(q_lo, q_hi, k_band):
    qs = jnp.concatenate([q_lo, q_hi], axis=0)
    return lax.dot_general(qs, k_band, (((1,), (1,)), ((), ())), preferred_element_type=F32)


def _pair_softmax_pv(s, v_band, bias, valid):
    rows = s.shape[0] // 2
    s = s + bias
    if valid is not None:
        s = jnp.where(valid, s, NEG_INF)
    m = jnp.max(s, axis=-1, keepdims=True)
    e = jnp.exp(s - m)
    o = jnp.dot(e.astype(BF16), v_band, preferred_element_type=F32)
    if v_band.shape[1] == V_EXT_W:
        o = o[:, :PAIR_W] * (1.0 / o[:, PAIR_W:])
    else:
        o = o * (1.0 / jnp.sum(e, axis=-1, keepdims=True))
    lane = lax.broadcasted_iota(jnp.int32, (rows, PAIR_W), 1)
    return jnp.where(lane < HEAD_DIM, o[:rows], o[rows:])


def _split_pair_queries(q):
    lane = lax.broadcasted_iota(jnp.int32, q.shape, 1)
    even = (lane % PAIR_W) < HEAD_DIM
    zero = jnp.zeros_like(q)
    return jnp.where(even, q, zero).astype(BF16), jnp.where(even, zero, q).astype(BF16)


def _short_conv(ubuf_ref, u, w_conv_ref, base, rows):
    ubuf_ref[base:base + rows, :] = u
    y = w_conv_ref[0:1, :] * ubuf_ref[base - 2:base - 2 + rows, :]
    y = y + w_conv_ref[1:2, :] * ubuf_ref[base - 1:base - 1 + rows, :]
    return y + w_conv_ref[2:3, :] * u


def _mixer_prompt_body(x_ref, gain_ref, win_ref, bg_ref, wconv_ref, bias_ref, wco_ref, wao_ref,
                       wo_ref, xo_ref, ko_ref, vo_ref, co_ref,
                       kbuf, vbuf, qlo, qhi, yattn, ubuf, sscr, *, conv_w, n_win):
    w = WINDOW
    d = x_ref.shape[-1]
    t = pl.program_id(1)
    tail_rows = CONV_K - 1
    q0 = 3 * conv_w
    g0 = q0 + 3 * ATTN_W

    @pl.when(t == 0)
    def _():
        kbuf[0:w, :] = jnp.zeros((w, ATTN_W), BF16)
        vlane = lax.broadcasted_iota(jnp.int32, vbuf.shape, 1)
        vbuf[...] = jnp.where((vlane % V_EXT_W) < PAIR_W, 0.0, 1.0).astype(BF16)
        ubuf[0:F32_SUBLANES, :] = jnp.zeros((F32_SUBLANES, conv_w), F32)

    first_valid_row = jnp.where(t > 0, 0, WINDOW)
    col = lax.broadcasted_iota(jnp.int32, (2 * CHUNK, BAND), 1)

    def project(wi):
        rows = slice(wi * w, (wi + 1) * w)
        x = x_ref[rows, :]
        h = _rmsnorm(x, gain_ref[...]).astype(BF16)
        zc = jnp.dot(h, win_ref[:, 0:q0], preferred_element_type=F32)
        u = zc[:, conv_w:2 * conv_w] * zc[:, 2 * conv_w:3 * conv_w]
        y = _short_conv(ubuf, u, wconv_ref, F32_SUBLANES + wi * w, w)
        yconv = (zc[:, 0:conv_w] * y).astype(BF16)
        zq = jnp.dot(h, win_ref[:, q0:g0], preferred_element_type=F32)
        k = zq[:, ATTN_W:2 * ATTN_W]
        v = zq[:, 2 * ATTN_W:3 * ATTN_W]
        if wi == n_win - 1:
            ko_ref[...] = k
            vo_ref[...] = v
            co_ref[...] = u[w - tail_rows:w, :]
        kv_rows = slice((wi + 1) * w, (wi + 2) * w)
        kbuf[kv_rows, :] = k.astype(BF16)
        v_bf = v.astype(BF16)
        for j in range(HEAD_PAIRS):
            vbuf[kv_rows, j * V_EXT_W:j * V_EXT_W + PAIR_W] = v_bf[:, j * PAIR_W:(j + 1) * PAIR_W]
        q_even, q_odd = _split_pair_queries(zq[:, 0:ATTN_W] * (HEAD_DIM ** -0.5))
        qlo[rows, :] = q_even
        qhi[rows, :] = q_odd
        return x, h, yconv

    def scores(item):
        wi, c, j = item
        r0 = wi * w + c * CHUNK
        lanes = slice(j * PAIR_W, (j + 1) * PAIR_W)
        return _pair_scores(qlo[r0:r0 + CHUNK, lanes], qhi[r0:r0 + CHUNK, lanes],
                            kbuf[r0:r0 + BAND, lanes])

    def attention(wi):
        items = [(wi, c, j) for c in range(w // CHUNK) for j in range(HEAD_PAIRS)]
        sscr[0] = scores(items[0])
        for i, (_, c, j) in enumerate(items):
            if i + 1 < len(items):
                sscr[(i + 1) % 2] = scores(items[i + 1])
            r0 = wi * w + c * CHUNK
            valid = ((col + c * CHUNK) >= first_valid_row) if wi == 0 else None
            out = _pair_softmax_pv(sscr[i % 2], vbuf[r0:r0 + BAND, j * V_EXT_W:(j + 1) * V_EXT_W],
                                   bias_ref[j], valid)
            yattn[r0:r0 + CHUNK, j * PAIR_W:(j + 1) * PAIR_W] = out.astype(BF16)

    def merge(wi, x, h, yconv):
        rows = slice(wi * w, (wi + 1) * w)
        gate = jax.nn.sigmoid(jnp.dot(h, win_ref[:, g0:g0 + 2 * d], preferred_element_type=F32)
                              + bg_ref[...])
        mc = jnp.dot(yconv, wco_ref[...], preferred_element_type=F32)
        ma = jnp.dot(yattn[rows, :], wao_ref[...], preferred_element_type=F32)
        m = gate[:, 0:d] * mc + gate[:, d:2 * d] * ma
        xo_ref[rows, :] = x + jnp.dot(m.astype(BF16), wo_ref[...], preferred_element_type=F32)

    for wi in range(n_win):
        projected = project(wi)
        attention(wi)
        merge(wi, *projected)
    kbuf[0:w, :] = kbuf[n_win * w:(n_win + 1) * w, :]
    vbuf[0:w, :] = vbuf[n_win * w:(n_win + 1) * w, :]
    ubuf[F32_SUBLANES - tail_rows:F32_SUBLANES, :] = (
        ubuf[F32_SUBLANES + n_win * w - tail_rows:F32_SUBLANES + n_win * w, :])


def _mixer_prompt(x, gain, w_in, b_gate, w_conv, bias, w_co, w_ao, w_o, *, layer, name):
    b, s, d = x.shape
    conv_w = w_conv.shape[-1]
    n_win = MIX_WINDOWS_PER_STEP
    tm = n_win * WINDOW
    assert s % tm == 0 and conv_w % 128 == 0
    weights = 2 * (w_in[0].size + w_co[0].size + w_ao[0].size + w_o[0].size) + 4 * bias.size
    est = (weights
           + 4 * tm * d * 4
           + 4 * WINDOW * ATTN_W * 4
           + (3 * (tm + WINDOW) + 3 * tm) * ATTN_W * 2
           + 2 * 2 * CHUNK * BAND * 4
           + (tm + F32_SUBLANES) * conv_w * 4
           + WINDOW * (3 * conv_w + 3 * ATTN_W + 4 * d + 3 * d) * 4
           + (8 << 20))
    row = lambda bi, ti: (bi, ti, 0)
    per_batch = lambda bi, ti: (bi, 0, 0)
    body = functools.partial(_mixer_prompt_body, conv_w=conv_w, n_win=n_win)
    return pl.pallas_call(
        body,
        grid=(b, s // tm),
        in_specs=[
            pl.BlockSpec((None, tm, d), row),
            _resident((1, d)),
            _layer_resident(w_in, layer),
            _resident(b_gate.shape),
            _resident(w_conv.shape),
            _resident(bias.shape),
            _layer_resident(w_co, layer),
            _layer_resident(w_ao, layer),
            _layer_resident(w_o, layer),
        ],
        out_specs=[
            pl.BlockSpec((None, tm, d), row),
            pl.BlockSpec((None, WINDOW, ATTN_W), per_batch),
            pl.BlockSpec((None, WINDOW, ATTN_W), per_batch),
            pl.BlockSpec((None, CONV_K - 1, conv_w), per_batch),
        ],
        out_shape=[
            jax.ShapeDtypeStruct((b, s, d), F32),
            jax.ShapeDtypeStruct((b, WINDOW, ATTN_W), F32),
            jax.ShapeDtypeStruct((b, WINDOW, ATTN_W), F32),
            jax.ShapeDtypeStruct((b, CONV_K - 1, conv_w), F32),
        ],
        scratch_shapes=[
            pltpu.VMEM((tm + WINDOW, ATTN_W), BF16),
            pltpu.VMEM((tm + WINDOW, HEAD_PAIRS * V_EXT_W), BF16),
            pltpu.VMEM((tm, ATTN_W), BF16),
            pltpu.VMEM((tm, ATTN_W), BF16),
            pltpu.VMEM((tm, ATTN_W), BF16),
            pltpu.VMEM((tm + F32_SUBLANES, conv_w), F32),
            pltpu.VMEM((2, 2 * CHUNK, BAND), F32),
        ],
        compiler_params=pltpu.CompilerParams(
            dimension_semantics=("arbitrary", "arbitrary"),
            vmem_limit_bytes=_vmem_limit(est)),
        name=name,
    )(x, gain, w_in, b_gate, w_conv, bias, w_co, w_ao, w_o)


def _mixer_sample_body(x_ref, gain_ref, win_ref, bg_ref, wconv_ref, bias_ref, wco_ref, wao_ref,
                       wo_ref, ck_ref, cv_ref, st_ref, xo_ref, ko_ref, vo_ref, co_ref,
                       kband, vband, yconv, yattn, ubuf, *, conv_w, n_streams, t_new, all_valid):
    d = x_ref.shape[-1]
    r_cache = ck_ref.shape[1]
    x = x_ref[...]
    h = _rmsnorm(x, gain_ref[...]).astype(BF16)

    zc = jnp.dot(h, win_ref[:, 0:3 * conv_w], preferred_element_type=F32)
    u = zc[:, conv_w:2 * conv_w] * zc[:, 2 * conv_w:3 * conv_w]
    cb = zc[:, 0:conv_w]
    q0 = 3 * conv_w
    zq = jnp.dot(h, win_ref[:, q0:q0 + 3 * ATTN_W], preferred_element_type=F32)
    k = zq[:, ATTN_W:2 * ATTN_W]
    v = zq[:, 2 * ATTN_W:3 * ATTN_W]
    ko_ref[...] = k
    vo_ref[...] = v
    q_even, q_odd = _split_pair_queries(zq[:, 0:ATTN_W] * (HEAD_DIM ** -0.5))
    valid = None
    if not all_valid:
        shape = (2 * t_new, r_cache + t_new)
        q_chunk = (PAST_LEN + lax.broadcasted_iota(jnp.int32, shape, 0) % t_new) // CHUNK
        k_pos = PAST_LEN - r_cache + lax.broadcasted_iota(jnp.int32, shape, 1)
        k_chunk = lax.shift_right_arithmetic(k_pos, CHUNK.bit_length() - 1)
        valid = (k_pos >= 0) & (k_chunk <= q_chunk) & (k_chunk >= q_chunk - N_PREV_CHUNKS)

    for bi in range(n_streams):
        rows = slice(bi * t_new, (bi + 1) * t_new)
        ubuf[F32_SUBLANES - (CONV_K - 1):F32_SUBLANES, :] = st_ref[bi]
        y = _short_conv(ubuf, u[rows], wconv_ref, F32_SUBLANES, t_new)
        yconv[rows, :] = (cb[rows] * y).astype(BF16)
        co_ref[bi] = ubuf[pl.ds(F32_SUBLANES + t_new - (CONV_K - 1), CONV_K - 1), :]

        kband[0:r_cache, :] = ck_ref[bi]
        vband[0:r_cache, :] = cv_ref[bi]
        kband[r_cache:r_cache + t_new, :] = k[rows].astype(BF16)
        vband[r_cache:r_cache + t_new, :] = v[rows].astype(BF16)
        for j in range(HEAD_PAIRS):
            lanes = slice(j * PAIR_W, (j + 1) * PAIR_W)
            s = _pair_scores(q_even[rows, lanes], q_odd[rows, lanes], kband[:, lanes])
            out = _pair_softmax_pv(s, vband[:, lanes], bias_ref[j], valid)
            yattn[rows, lanes] = out.astype(BF16)

    g0 = q0 + 3 * ATTN_W
    gate = jax.nn.sigmoid(jnp.dot(h, win_ref[:, g0:g0 + 2 * d], preferred_element_type=F32)
                          + bg_ref[...])
    mc = jnp.dot(yconv[...], wco_ref[...], preferred_element_type=F32)
    ma = jnp.dot(yattn[...], wao_ref[...], preferred_element_type=F32)
    m = gate[:, 0:d] * mc + gate[:, d:2 * d] * ma
    xo_ref[...] = x + jnp.dot(m.astype(BF16), wo_ref[...], preferred_element_type=F32)


def _mixer_sample(x2d, gain, w_in, b_gate, w_conv, bias, w_co, w_ao, w_o, cache_k, cache_v,
                  state, all_valid, *, layer, name):
    n, d = x2d.shape
    _, n_streams, r_cache, _ = cache_k.shape
    t_new = n // n_streams
    conv_w = w_conv.shape[-1]
    n_keys = r_cache + t_new
    assert t_new % 16 == 0 and r_cache % 16 == 0
    stacked = (w_in, w_co, w_ao, w_o, cache_k, cache_v)
    operands = (x2d, gain, w_in, b_gate, w_conv, bias, w_co, w_ao, w_o, cache_k, cache_v, state)
    est = (sum(a[0].size * a.dtype.itemsize if any(a is w for w in stacked)
               else a.size * a.dtype.itemsize for a in operands)
           + 2 * (n * d + 2 * n * ATTN_W + n_streams * (CONV_K - 1) * conv_w) * 4
           + (2 * n_keys + 2 * n) * ATTN_W * 2
           + n * (3 * conv_w + 3 * ATTN_W + 4 * d + 3 * d) * 4
           + (8 << 20))
    body = functools.partial(_mixer_sample_body, conv_w=conv_w, n_streams=n_streams,
                             t_new=t_new, all_valid=all_valid)
    full = lambda shape: pl.BlockSpec(shape, lambda i: (0,) * len(shape))
    return pl.pallas_call(
        body,
        grid=(1,),
        in_specs=[_layer_resident(a, layer) if any(a is w for w in stacked) else _resident(a.shape)
                  for a in operands],
        out_specs=[
            full((n, d)),
            full((n, ATTN_W)),
            full((n, ATTN_W)),
            full((n_streams, CONV_K - 1, conv_w)),
        ],
        out_shape=[
            jax.ShapeDtypeStruct((n, d), F32),
            jax.ShapeDtypeStruct((n, ATTN_W), F32),
            jax.ShapeDtypeStruct((n, ATTN_W), F32),
            jax.ShapeDtypeStruct((n_streams, CONV_K - 1, conv_w), F32),
        ],
        scratch_shapes=[
            pltpu.VMEM((n_keys, ATTN_W), BF16),
            pltpu.VMEM((n_keys, ATTN_W), BF16),
            pltpu.VMEM((n, conv_w), BF16),
            pltpu.VMEM((n, ATTN_W), BF16),
            pltpu.VMEM((t_new + F32_SUBLANES, conv_w), F32),
        ],
        compiler_params=pltpu.CompilerParams(
            dimension_semantics=("arbitrary",),
            vmem_limit_bytes=_vmem_limit(est)),
        name=name,
    )(*operands)


def _band_bias(rel_bias_l, offset, n_q, n_k):
    period = n_k + n_q
    m = np.arange(period)
    m = np.where(m < n_k, m, m - period)
    idx = np.clip(offset - m, -REL_CLIP, REL_CLIP) + REL_CLIP
    vec = rel_bias_l[:, idx]
    h = vec.shape[0]
    flat = jnp.tile(vec, (1, n_q))[:, :n_q * (period - 1)]
    tiles = flat.reshape(h, n_q, period - 1)[:, :, :n_k]
    return tiles.reshape(h // 2, 2 * n_q, n_k)


def _band_valid(q_pos, k_pos):
    qc = q_pos[:, None] // CHUNK
    kc = k_pos[None, :] // CHUNK
    return (k_pos[None, :] >= 0) & (kc <= qc) & (kc >= qc - N_PREV_CHUNKS)


def kernel(x_prompt, x_sample, cache_k, cache_v, state_conv, norm_ffn1, w_ffn1_gu, w_ffn1_down,
           norm_mix, w_in, b_gate, w_conv, rel_bias, w_conv_out, w_attn_out, w_o, norm_ffn2,
           w_ffn2_gu, w_ffn2_down, norm_final):
    depth = w_in.shape[0]
    b, s, d = x_prompt.shape
    nb, ts, _ = x_sample.shape
    r_cache = cache_k.shape[2]
    assert min(WINDOW, s) == WINDOW

    xp = x_prompt.reshape(b * s, d)
    xs = x_sample.reshape(nb * ts, d)
    w1gu, w1d = w_ffn1_gu.astype(BF16), w_ffn1_down.astype(BF16)
    w2gu, w2d = w_ffn2_gu.astype(BF16), w_ffn2_down.astype(BF16)
    win, wco, wao, wo = (w_in.astype(BF16), w_conv_out.astype(BF16), w_attn_out.astype(BF16),
                         w_o.astype(BF16))
    ck = cache_k.astype(BF16).reshape(depth, nb, r_cache, ATTN_W)
    cv = cache_v.astype(BF16).reshape(depth, nb, r_cache, ATTN_W)
    gfin = norm_final.reshape(1, d)

    qs_pos = PAST_LEN + np.arange(ts)
    ks_pos = np.concatenate([PAST_LEN - r_cache + np.arange(r_cache), qs_pos])
    all_valid_s = bool(_band_valid(qs_pos, ks_pos).all())

    outs = {n: [] for n in ("kp", "vp", "cp", "ks", "vs", "cs")}
    for l in range(depth):
        last = l == depth - 1
        g1 = norm_ffn1[l].reshape(1, d)
        gm = norm_mix[l].reshape(1, d)
        g2 = norm_ffn2[l].reshape(1, d)
        bg = b_gate[l].reshape(1, 2 * d)
        bias_p = _band_bias(rel_bias[l], WINDOW, CHUNK, BAND)
        bias_s = _band_bias(rel_bias[l], r_cache, ts, r_cache + ts)

        xp, xs = _ffn(xp, xs, g1, w1gu, w1d, gfin, layer=l, final_norm=False, name=f"ffn1_l{l}")

        xp3, kp, vp, cp = _mixer_prompt(xp.reshape(b, s, d), gm, win, bg, w_conv[l], bias_p,
                                        wco, wao, wo, layer=l, name=f"mixer_prompt_l{l}")
        xp = xp3.reshape(b * s, d)
        xs, ks, vs, cs = _mixer_sample(xs, gm, win, bg, w_conv[l], bias_s, wco, wao, wo, ck, cv,
                                       state_conv[l], all_valid_s, layer=l,
                                       name=f"mixer_sample_l{l}")

        xp, xs = _ffn(xp, xs, g2, w2gu, w2d, gfin, layer=l, final_norm=last, name=f"ffn2_l{l}")

        outs["kp"].append(kp.reshape(b, WINDOW, N_HEADS, HEAD_DIM))
        outs["vp"].append(vp.reshape(b, WINDOW, N_HEADS, HEAD_DIM))
        outs["cp"].append(cp)
        outs["ks"].append(ks.reshape(nb, ts, N_HEADS, HEAD_DIM))
        outs["vs"].append(vs.reshape(nb, ts, N_HEADS, HEAD_DIM))
        outs["cs"].append(cs)

    stack = lambda name: jnp.stack(outs[name], axis=0)
    return (xp.reshape(b, s, d), xs.reshape(nb, ts, d), stack("kp"), stack("vp"), stack("cp"),
            stack("ks"), stack("vs"), stack("cs"))
```

```python
import functools

import numpy as np
import jax
import jax.numpy as jnp
from jax import lax
from jax.experimental import pallas as pl
from jax.experimental.pallas import tpu as pltpu

CHUNK = 64
N_PREV_CHUNKS = 8
WINDOW = N_PREV_CHUNKS * CHUNK
BAND = WINDOW + CHUNK
N_HEADS = 8
HEAD_DIM = 64
ATTN_W = N_HEADS * HEAD_DIM
HEAD_PAIRS = N_HEADS // 2
PAIR_W = 2 * HEAD_DIM
V_EXT_W = 2 * PAIR_W
REL_CLIP = 128
CONV_K = 3
PAST_LEN = 2048
EPS = 1e-6
NEG_INF = -1e30
LOG2_E = 1.4426950408889634

F32 = jnp.float32
BF16 = jnp.bfloat16

V7X_SCOPED_VMEM_MAX_BYTES = 60000 * 1024
V7X_MXU_DIM = 256
F32_SUBLANES = 8

MIX_WINDOWS_PER_STEP = 2
ATT_GROUP = 2
GROUP_Q = ATT_GROUP * CHUNK
GROUP_K = WINDOW + GROUP_Q
FFN_TOKEN_TILE = 1024
FFN_SUB_ROWS = 256
FFN_CHUNK_MAX = 1536


def _resident(shape):
    zeros = (0,) * len(shape)
    return pl.BlockSpec(shape, lambda *_: zeros, pipeline_mode=pl.Buffered(1))


def _layer_resident(stacked, layer):
    tail = (0,) * (stacked.ndim - 1)
    return pl.BlockSpec((None,) + stacked.shape[1:], lambda *_: (layer,) + tail,
                        pipeline_mode=pl.Buffered(1))


def _vmem_limit(estimate_bytes):
    return int(min(V7X_SCOPED_VMEM_MAX_BYTES, estimate_bytes))


def _rmsnorm(x, gain):
    return x * lax.rsqrt(jnp.mean(x * x, axis=-1, keepdims=True) + EPS) * gain


def _ff_chunks(d_ff):
    chunks = []
    left = d_ff
    while left > 0:
        ck = min(FFN_CHUNK_MAX, left)
        chunks.append(ck)
        left -= ck
    assert all(c % V7X_MXU_DIM == 0 for c in chunks), chunks
    return tuple(chunks)


def _ffn_rows(x, gain_ref, wgu_ref, wd_ref, gfin_ref, *, chunks, d_ff, final_norm):
    h = _rmsnorm(x, gain_ref[...]).astype(BF16)
    acc = None
    c0 = 0
    for ck in chunks:
        g = jnp.dot(h, wgu_ref[:, c0:c0 + ck], preferred_element_type=F32)
        u = jnp.dot(h, wgu_ref[:, d_ff + c0:d_ff + c0 + ck], preferred_element_type=F32)
        a = (jax.nn.silu(g) * u).astype(BF16)
        d = jnp.dot(a, wd_ref[c0:c0 + ck, :], preferred_element_type=F32)
        acc = d if acc is None else acc + d
        c0 += ck
    y = x + 0.5 * acc
    if final_norm:
        y = _rmsnorm(y, gfin_ref[...])
    return y


def _ffn_body(x_ref, xs_ref, gain_ref, wgu_ref, wd_ref, gfin_ref, o_ref, os_ref, *, sub_rows,
              **static):
    weights = (gain_ref, wgu_ref, wd_ref, gfin_ref)
    for r0 in range(0, x_ref.shape[0], sub_rows):
        o_ref[r0:r0 + sub_rows, :] = _ffn_rows(x_ref[r0:r0 + sub_rows, :], *weights, **static)

    @pl.when(pl.program_id(0) == pl.num_programs(0) - 1)
    def _():
        os_ref[...] = _ffn_rows(xs_ref[...], *weights, **static)


def _ffn(x2d, xs2d, gain, w_gu, w_down, gain_final, *, layer, final_norm, name):
    n, d = x2d.shape
    ns = xs2d.shape[0]
    d_ff = w_down.shape[1]
    tm = min(FFN_TOKEN_TILE, n)
    sub = min(FFN_SUB_ROWS, tm)
    assert n % tm == 0 and tm % sub == 0 and sub % F32_SUBLANES == 0 and ns % F32_SUBLANES == 0
    chunks = _ff_chunks(d_ff)
    ck = max(chunks)
    est = (2 * (w_gu[0].size + w_down[0].size)
           + 4 * tm * d * 4 + 3 * ns * d * 4
           + 2 * sub * (2 * ck * 4 + ck * 2 + d * 2 + 3 * d * 4)
           + (8 << 20))
    body = functools.partial(_ffn_body, sub_rows=sub, chunks=chunks, d_ff=d_ff,
                             final_norm=final_norm)
    return pl.pallas_call(
        body,
        grid=(n // tm,),
        in_specs=[
            pl.BlockSpec((tm, d), lambda i: (i, 0)),
            _resident((ns, d)),
            _resident((1, d)),
            _layer_resident(w_gu, layer),
            _layer_resident(w_down, layer),
            _resident((1, d)),
        ],
        out_specs=[pl.BlockSpec((tm, d), lambda i: (i, 0)),
                   pl.BlockSpec((ns, d), lambda i: (0, 0))],
        out_shape=[jax.ShapeDtypeStruct((n, d), F32), jax.ShapeDtypeStruct((ns, d), F32)],
        compiler_params=pltpu.CompilerParams(
            dimension_semantics=("arbitrary",),
            vmem_limit_bytes=_vmem_limit(est)),
        name=name,
    )(x2d, xs2d, gain, w_gu, w_down, gain_final)


def _pair_scores(q_lo, q_hi, k_band):
    qs = jnp.concatenate([q_lo, q_hi], axis=0)
    return lax.dot_general(qs, k_band, (((1,), (1,)), ((), ())), preferred_element_type=F32)


def _pair_softmax_pv(s, v_band, bias, valid):
    rows = s.shape[0] // 2
    s = s + bias
    if valid is not None:
        s = jnp.where(valid, s, NEG_INF)
    m = jnp.max(s, axis=-1, keepdims=True)
    e = jnp.exp2(s - m)
    o = jnp.dot(e.astype(BF16), v_band, preferred_element_type=F32)
    if v_band.shape[1] == V_EXT_W:
        o = o[:, :PAIR_W] * (1.0 / o[:, PAIR_W:])
    else:
        o = o * (1.0 / jnp.sum(e, axis=-1, keepdims=True))
    lane = lax.broadcasted_iota(jnp.int32, (rows, PAIR_W), 1)
    return jnp.where(lane < HEAD_DIM, o[:rows], o[rows:])


def _split_pair_queries(q):
    lane = lax.broadcasted_iota(jnp.int32, q.shape, 1)
    even = (lane % PAIR_W) < HEAD_DIM
    zero = jnp.zeros_like(q)
    return jnp.where(even, q, zero).astype(BF16), jnp.where(even, zero, q).astype(BF16)


def _short_conv(ubuf_ref, u, w_conv_ref, base, rows):
    ubuf_ref[base:base + rows, :] = u
    y = w_conv_ref[0:1, :] * ubuf_ref[base - 2:base - 2 + rows, :]
    y = y + w_conv_ref[1:2, :] * ubuf_ref[base - 1:base - 1 + rows, :]
    return y + w_conv_ref[2:3, :] * u


def _mixer_prompt_body(x_ref, gain_ref, win_ref, bg_ref, wconv_ref, bias_ref, wco_ref, wao_ref,
                       wo_ref, xo_ref, ko_ref, vo_ref, co_ref,
                       kbuf, vbuf, qlo, qhi, yattn, ubuf, sscr, *, conv_w, n_win):
    w = WINDOW
    d = x_ref.shape[-1]
    t = pl.program_id(1)
    tail_rows = CONV_K - 1
    q0 = 3 * conv_w
    g0 = q0 + 3 * ATTN_W

    @pl.when(t == 0)
    def _():
        kbuf[0:w, :] = jnp.zeros((w, ATTN_W), BF16)
        vlane = lax.broadcasted_iota(jnp.int32, vbuf.shape, 1)
        vbuf[...] = jnp.where((vlane % V_EXT_W) < PAIR_W, 0.0, 1.0).astype(BF16)
        ubuf[0:F32_SUBLANES, :] = jnp.zeros((F32_SUBLANES, conv_w), F32)

    first_valid_row = jnp.where(t > 0, 0, WINDOW)
    col = lax.broadcasted_iota(jnp.int32, (2 * GROUP_Q, GROUP_K), 1)

    def project(wi):
        rows = slice(wi * w, (wi + 1) * w)
        x = x_ref[rows, :]
        h = _rmsnorm(x, gain_ref[...]).astype(BF16)
        zc = jnp.dot(h, win_ref[:, 0:q0], preferred_element_type=F32)
        u = zc[:, conv_w:2 * conv_w] * zc[:, 2 * conv_w:3 * conv_w]
        y = _short_conv(ubuf, u, wconv_ref, F32_SUBLANES + wi * w, w)
        yconv = (zc[:, 0:conv_w] * y).astype(BF16)
        zq = jnp.dot(h, win_ref[:, q0:g0], preferred_element_type=F32)
        k = zq[:, ATTN_W:2 * ATTN_W]
        v = zq[:, 2 * ATTN_W:3 * ATTN_W]
        if wi == n_win - 1:
            ko_ref[...] = k
            vo_ref[...] = v
            co_ref[...] = u[w - tail_rows:w, :]
        kv_rows = slice((wi + 1) * w, (wi + 2) * w)
        kbuf[kv_rows, :] = k.astype(BF16)
        v_bf = v.astype(BF16)
        for j in range(HEAD_PAIRS):
            vbuf[kv_rows, j * V_EXT_W:j * V_EXT_W + PAIR_W] = v_bf[:, j * PAIR_W:(j + 1) * PAIR_W]
        q_even, q_odd = _split_pair_queries(zq[:, 0:ATTN_W] * (LOG2_E * HEAD_DIM ** -0.5))
        qlo[rows, :] = q_even
        qhi[rows, :] = q_odd
        return x, h, yconv

    def scores(item):
        wi, c, j = item
        r0 = wi * w + c * GROUP_Q
        lanes = slice(j * PAIR_W, (j + 1) * PAIR_W)
        return _pair_scores(qlo[r0:r0 + GROUP_Q, lanes], qhi[r0:r0 + GROUP_Q, lanes],
                            kbuf[r0:r0 + GROUP_K, lanes])

    def attention(wi):
        items = [(wi, c, j) for c in range(w // GROUP_Q) for j in range(HEAD_PAIRS)]
        sscr[0] = scores(items[0])
        for i, (_, c, j) in enumerate(items):
            if i + 1 < len(items):
                sscr[(i + 1) % 2] = scores(items[i + 1])
            r0 = wi * w + c * GROUP_Q
            valid = ((col + c * GROUP_Q) >= first_valid_row) if wi == 0 else None
            out = _pair_softmax_pv(sscr[i % 2],
                                   vbuf[r0:r0 + GROUP_K, j * V_EXT_W:(j + 1) * V_EXT_W],
                                   bias_ref[j], valid)
            yattn[r0:r0 + GROUP_Q, j * PAIR_W:(j + 1) * PAIR_W] = out.astype(BF16)

    def merge(wi, x, h, yconv):
        rows = slice(wi * w, (wi + 1) * w)
        gate = jax.nn.sigmoid(jnp.dot(h, win_ref[:, g0:g0 + 2 * d], preferred_element_type=F32)
                              + bg_ref[...])
        mc = jnp.dot(yconv, wco_ref[...], preferred_element_type=F32)
        ma = jnp.dot(yattn[rows, :], wao_ref[...], preferred_element_type=F32)
        m = gate[:, 0:d] * mc + gate[:, d:2 * d] * ma
        xo_ref[rows, :] = x + jnp.dot(m.astype(BF16), wo_ref[...], preferred_element_type=F32)

    for wi in range(n_win):
        projected = project(wi)
        attention(wi)
        merge(wi, *projected)
    kbuf[0:w, :] = kbuf[n_win * w:(n_win + 1) * w, :]
    vbuf[0:w, :] = vbuf[n_win * w:(n_win + 1) * w, :]
    ubuf[F32_SUBLANES - tail_rows:F32_SUBLANES, :] = (
        ubuf[F32_SUBLANES + n_win * w - tail_rows:F32_SUBLANES + n_win * w, :])


def _mixer_prompt(x, gain, w_in, b_gate, w_conv, bias, w_co, w_ao, w_o, *, layer, name):
    b, s, d = x.shape
    conv_w = w_conv.shape[-1]
    n_win = MIX_WINDOWS_PER_STEP
    tm = n_win * WINDOW
    assert s % tm == 0 and conv_w % 128 == 0
    weights = 2 * (w_in[0].size + w_co[0].size + w_ao[0].size + w_o[0].size) + 4 * bias.size
    est = (weights
           + 4 * tm * d * 4
           + 4 * WINDOW * ATTN_W * 4
           + (3 * (tm + WINDOW) + 3 * tm) * ATTN_W * 2
           + 2 * 2 * GROUP_Q * GROUP_K * 4
           + (tm + F32_SUBLANES) * conv_w * 4
           + WINDOW * (3 * conv_w + 3 * ATTN_W + 4 * d + 3 * d) * 4
           + (8 << 20))
    row = lambda bi, ti: (bi, ti, 0)
    per_batch = lambda bi, ti: (bi, 0, 0)
    body = functools.partial(_mixer_prompt_body, conv_w=conv_w, n_win=n_win)
    return pl.pallas_call(
        body,
        grid=(b, s // tm),
        in_specs=[
            pl.BlockSpec((None, tm, d), row),
            _resident((1, d)),
            _layer_resident(w_in, layer),
            _resident(b_gate.shape),
            _resident(w_conv.shape),
            _resident(bias.shape),
            _layer_resident(w_co, layer),
            _layer_resident(w_ao, layer),
            _layer_resident(w_o, layer),
        ],
        out_specs=[
            pl.BlockSpec((None, tm, d), row),
            pl.BlockSpec((None, WINDOW, ATTN_W), per_batch),
            pl.BlockSpec((None, WINDOW, ATTN_W), per_batch),
            pl.BlockSpec((None, CONV_K - 1, conv_w), per_batch),
        ],
        out_shape=[
            jax.ShapeDtypeStruct((b, s, d), F32),
            jax.ShapeDtypeStruct((b, WINDOW, ATTN_W), F32),
            jax.ShapeDtypeStruct((b, WINDOW, ATTN_W), F32),
            jax.ShapeDtypeStruct((b, CONV_K - 1, conv_w), F32),
        ],
        scratch_shapes=[
            pltpu.VMEM((tm + WINDOW, ATTN_W), BF16),
            pltpu.VMEM((tm + WINDOW, HEAD_PAIRS * V_EXT_W), BF16),
            pltpu.VMEM((tm, ATTN_W), BF16),
            pltpu.VMEM((tm, ATTN_W), BF16),
            pltpu.VMEM((tm, ATTN_W), BF16),
            pltpu.VMEM((tm + F32_SUBLANES, conv_w), F32),
            pltpu.VMEM((2, 2 * GROUP_Q, GROUP_K), F32),
        ],
        compiler_params=pltpu.CompilerParams(
            dimension_semantics=("arbitrary", "arbitrary"),
            vmem_limit_bytes=_vmem_limit(est)),
        name=name,
    )(x, gain, w_in, b_gate, w_conv, bias, w_co, w_ao, w_o)


def _mixer_sample_body(x_ref, gain_ref, win_ref, bg_ref, wconv_ref, bias_ref, wco_ref, wao_ref,
                       wo_ref, ck_ref, cv_ref, st_ref, xo_ref, ko_ref, vo_ref, co_ref,
                       kband, vband, yconv, yattn, ubuf, *, conv_w, n_streams, t_new, all_valid):
    d = x_ref.shape[-1]
    r_cache = ck_ref.shape[1]
    x = x_ref[...]
    h = _rmsnorm(x, gain_ref[...]).astype(BF16)

    zc = jnp.dot(h, win_ref[:, 0:3 * conv_w], preferred_element_type=F32)
    u = zc[:, conv_w:2 * conv_w] * zc[:, 2 * conv_w:3 * conv_w]
    cb = zc[:, 0:conv_w]
    q0 = 3 * conv_w
    zq = jnp.dot(h, win_ref[:, q0:q0 + 3 * ATTN_W], preferred_element_type=F32)
    k = zq[:, ATTN_W:2 * ATTN_W]
    v = zq[:, 2 * ATTN_W:3 * ATTN_W]
    ko_ref[...] = k
    vo_ref[...] = v
    q_even, q_odd = _split_pair_queries(zq[:, 0:ATTN_W] * (LOG2_E * HEAD_DIM ** -0.5))
    valid = None
    if not all_valid:
        shape = (2 * t_new, r_cache + t_new)
        q_chunk = (PAST_LEN + lax.broadcasted_iota(jnp.int32, shape, 0) % t_new) // CHUNK
        k_pos = PAST_LEN - r_cache + lax.broadcasted_iota(jnp.int32, shape, 1)
        k_chunk = lax.shift_right_arithmetic(k_pos, CHUNK.bit_length() - 1)
        valid = (k_pos >= 0) & (k_chunk <= q_chunk) & (k_chunk >= q_chunk - N_PREV_CHUNKS)

    for bi in range(n_streams):
        rows = slice(bi * t_new, (bi + 1) * t_new)
        ubuf[F32_SUBLANES - (CONV_K - 1):F32_SUBLANES, :] = st_ref[bi]
        y = _short_conv(ubuf, u[rows], wconv_ref, F32_SUBLANES, t_new)
        yconv[rows, :] = (cb[rows] * y).astype(BF16)
        co_ref[bi] = ubuf[pl.ds(F32_SUBLANES + t_new - (CONV_K - 1), CONV_K - 1), :]

        kband[0:r_cache, :] = ck_ref[bi]
        vband[0:r_cache, :] = cv_ref[bi]
        kband[r_cache:r_cache + t_new, :] = k[rows].astype(BF16)
        vband[r_cache:r_cache + t_new, :] = v[rows].astype(BF16)
        for j in range(HEAD_PAIRS):
            lanes = slice(j * PAIR_W, (j + 1) * PAIR_W)
            s = _pair_scores(q_even[rows, lanes], q_odd[rows, lanes], kband[:, lanes])
            out = _pair_softmax_pv(s, vband[:, lanes], bias_ref[j], valid)
            yattn[rows, lanes] = out.astype(BF16)

    g0 = q0 + 3 * ATTN_W
    gate = jax.nn.sigmoid(jnp.dot(h, win_ref[:, g0:g0 + 2 * d], preferred_element_type=F32)
                          + bg_ref[...])
    mc = jnp.dot(yconv[...], wco_ref[...], preferred_element_type=F32)
    ma = jnp.dot(yattn[...], wao_ref[...], preferred_element_type=F32)
    m = gate[:, 0:d] * mc + gate[:, d:2 * d] * ma
    xo_ref[...] = x + jnp.dot(m.astype(BF16), wo_ref[...], preferred_element_type=F32)


def _mixer_sample(x2d, gain, w_in, b_gate, w_conv, bias, w_co, w_ao, w_o, cache_k, cache_v,
                  state, all_valid, *, layer, name):
    n, d = x2d.shape
    _, n_streams, r_cache, _ = cache_k.shape
    t_new = n // n_streams
    conv_w = w_conv.shape[-1]
    n_keys = r_cache + t_new
    assert t_new % 16 == 0 and r_cache % 16 == 0
    stacked = (w_in, w_co, w_ao, w_o, cache_k, cache_v)
    operands = (x2d, gain, w_in, b_gate, w_conv, bias, w_co, w_ao, w_o, cache_k, cache_v, state)
    est = (sum(a[0].size * a.dtype.itemsize if any(a is w for w in stacked)
               else a.size * a.dtype.itemsize for a in operands)
           + 2 * (n * d + 2 * n * ATTN_W + n_streams * (CONV_K - 1) * conv_w) * 4
           + (2 * n_keys + 2 * n) * ATTN_W * 2
           + n * (3 * conv_w + 3 * ATTN_W + 4 * d + 3 * d) * 4
           + (8 << 20))
    body = functools.partial(_mixer_sample_body, conv_w=conv_w, n_streams=n_streams,
                             t_new=t_new, all_valid=all_valid)
    full = lambda shape: pl.BlockSpec(shape, lambda i: (0,) * len(shape))
    return pl.pallas_call(
        body,
        grid=(1,),
        in_specs=[_layer_resident(a, layer) if any(a is w for w in stacked) else _resident(a.shape)
                  for a in operands],
        out_specs=[
            full((n, d)),
            full((n, ATTN_W)),
            full((n, ATTN_W)),
            full((n_streams, CONV_K - 1, conv_w)),
        ],
        out_shape=[
            jax.ShapeDtypeStruct((n, d), F32),
            jax.ShapeDtypeStruct((n, ATTN_W), F32),
            jax.ShapeDtypeStruct((n, ATTN_W), F32),
            jax.ShapeDtypeStruct((n_streams, CONV_K - 1, conv_w), F32),
        ],
        scratch_shapes=[
            pltpu.VMEM((n_keys, ATTN_W), BF16),
            pltpu.VMEM((n_keys, ATTN_W), BF16),
            pltpu.VMEM((n, conv_w), BF16),
            pltpu.VMEM((n, ATTN_W), BF16),
            pltpu.VMEM((t_new + F32_SUBLANES, conv_w), F32),
        ],
        compiler_params=pltpu.CompilerParams(
            dimension_semantics=("arbitrary",),
            vmem_limit_bytes=_vmem_limit(est)),
        name=name,
    )(*operands)


def _band_bias(rel_bias_l, offset, n_q, n_k):
    period = n_k + n_q
    m = np.arange(period)
    m = np.where(m < n_k, m, m - period)
    idx = np.clip(offset - m, -REL_CLIP, REL_CLIP) + REL_CLIP
    vec = rel_bias_l[:, idx]
    h = vec.shape[0]
    flat = jnp.tile(vec, (1, n_q))[:, :n_q * (period - 1)]
    tiles = flat.reshape(h, n_q, period - 1)[:, :, :n_k]
    return tiles.reshape(h // 2, 2 * n_q, n_k)


def _group_bias(rel_bias_l):
    base = _band_bias(rel_bias_l, WINDOW, GROUP_Q, GROUP_K)
    first_key = (np.arange(GROUP_Q)[:, None] // CHUNK) * CHUNK
    key = np.arange(GROUP_K)[None, :]
    in_band = (key >= first_key) & (key < first_key + BAND)
    if in_band.all():
        return base
    return jnp.where(np.concatenate([in_band, in_band], axis=0)[None], base, NEG_INF)


def _band_valid(q_pos, k_pos):
    qc = q_pos[:, None] // CHUNK
    kc = k_pos[None, :] // CHUNK
    return (k_pos[None, :] >= 0) & (kc <= qc) & (kc >= qc - N_PREV_CHUNKS)


def kernel(x_prompt, x_sample, cache_k, cache_v, state_conv, norm_ffn1, w_ffn1_gu, w_ffn1_down,
           norm_mix, w_in, b_gate, w_conv, rel_bias, w_conv_out, w_attn_out, w_o, norm_ffn2,
           w_ffn2_gu, w_ffn2_down, norm_final):
    depth = w_in.shape[0]
    b, s, d = x_prompt.shape
    nb, ts, _ = x_sample.shape
    r_cache = cache_k.shape[2]
    assert min(WINDOW, s) == WINDOW

    xp = x_prompt.reshape(b * s, d)
    xs = x_sample.reshape(nb * ts, d)
    w1gu, w1d = w_ffn1_gu.astype(BF16), w_ffn1_down.astype(BF16)
    w2gu, w2d = w_ffn2_gu.astype(BF16), w_ffn2_down.astype(BF16)
    win, wco, wao, wo = (w_in.astype(BF16), w_conv_out.astype(BF16), w_attn_out.astype(BF16),
                         w_o.astype(BF16))
    ck = cache_k.astype(BF16).reshape(depth, nb, r_cache, ATTN_W)
    cv = cache_v.astype(BF16).reshape(depth, nb, r_cache, ATTN_W)
    gfin = norm_final.reshape(1, d)

    qs_pos = PAST_LEN + np.arange(ts)
    ks_pos = np.concatenate([PAST_LEN - r_cache + np.arange(r_cache), qs_pos])
    all_valid_s = bool(_band_valid(qs_pos, ks_pos).all())

    outs = {n: [] for n in ("kp", "vp", "cp", "ks", "vs", "cs")}
    for l in range(depth):
        last = l == depth - 1
        g1 = norm_ffn1[l].reshape(1, d)
        gm = norm_mix[l].reshape(1, d)
        g2 = norm_ffn2[l].reshape(1, d)
        bg = b_gate[l].reshape(1, 2 * d)
        bias_p = _group_bias(rel_bias[l] * LOG2_E)
        bias_s = _band_bias(rel_bias[l] * LOG2_E, r_cache, ts, r_cache + ts)

        xp, xs = _ffn(xp, xs, g1, w1gu, w1d, gfin, layer=l, final_norm=False, name=f"ffn1_l{l}")

        xp3, kp, vp, cp = _mixer_prompt(xp.reshape(b, s, d), gm, win, bg, w_conv[l], bias_p,
                                        wco, wao, wo, layer=l, name=f"mixer_prompt_l{l}")
        xp = xp3.reshape(b * s, d)
        xs, ks, vs, cs = _mixer_sample(xs, gm, win, bg, w_conv[l], bias_s, wco, wao, wo, ck, cv,
                                       state_conv[l], all_valid_s, layer=l,
                                       name=f"mixer_sample_l{l}")

        xp, xs = _ffn(xp, xs, g2, w2gu, w2d, gfin, layer=l, final_norm=last, name=f"ffn2_l{l}")

        outs["kp"].append(kp.reshape(b, WINDOW, N_HEADS, HEAD_DIM))
        outs["vp"].append(vp.reshape(b, WINDOW, N_HEADS, HEAD_DIM))
        outs["cp"].append(cp)
        outs["ks"].append(ks.reshape(nb, ts, N_HEADS, HEAD_DIM))
        outs["vs"].append(vs.reshape(nb, ts, N_HEADS, HEAD_DIM))
        outs["cs"].append(cs)

    stack = lambda name: jnp.stack(outs[name], axis=0)
    return (xp.reshape(b, s, d), xs.reshape(nb, ts, d), stack("kp"), stack("vp"), stack("cp"),
            stack("ks"), stack("vs"), stack("cs"))
```

```python
import functools

import numpy as np
import jax
import jax.numpy as jnp
from jax import lax
from jax.experimental import pallas as pl
from jax.experimental.pallas import tpu as pltpu

CHUNK = 64
N_PREV_CHUNKS = 8
WINDOW = N_PREV_CHUNKS * CHUNK
BAND = WINDOW + CHUNK
N_HEADS = 8
HEAD_DIM = 64
ATTN_W = N_HEADS * HEAD_DIM
HEAD_PAIRS = N_HEADS // 2
PAIR_W = 2 * HEAD_DIM
V_EXT_W = 2 * PAIR_W
REL_CLIP = 128
CONV_K = 3
PAST_LEN = 2048
EPS = 1e-6
NEG_INF = -1e30
LOG2_E = 1.4426950408889634

F32 = jnp.float32
BF16 = jnp.bfloat16

V7X_SCOPED_VMEM_MAX_BYTES = 60000 * 1024
V7X_MXU_DIM = 256
F32_SUBLANES = 8

MIX_WINDOWS_PER_STEP = 2
ATT_GROUP = 2
GROUP_Q = ATT_GROUP * CHUNK
GROUP_K = WINDOW + GROUP_Q
FFN_TOKEN_TILE = 1024
FFN_SUB_ROWS = 256
FFN_CHUNK_MAX = 1536
FFN_CAST_STEPS = 4


def _resident(shape):
    zeros = (0,) * len(shape)
    return pl.BlockSpec(shape, lambda *_: zeros, pipeline_mode=pl.Buffered(1))


def _layer_resident(stacked, layer):
    tail = (0,) * (stacked.ndim - 1)
    return pl.BlockSpec((None,) + stacked.shape[1:], lambda *_: (layer,) + tail,
                        pipeline_mode=pl.Buffered(1))


def _vmem_limit(estimate_bytes):
    return int(min(V7X_SCOPED_VMEM_MAX_BYTES, estimate_bytes))


def _rmsnorm(x, gain):
    return x * lax.rsqrt(jnp.mean(x * x, axis=-1, keepdims=True) + EPS) * gain


def _ff_chunks(d_ff):
    chunks = []
    left = d_ff
    while left > 0:
        ck = min(FFN_CHUNK_MAX, left)
        chunks.append(ck)
        left -= ck
    assert all(c % V7X_MXU_DIM == 0 for c in chunks), chunks
    return tuple(chunks)


def _ffn_rows(x, gain_ref, wgu_ref, wd_ref, gfin_ref, *, chunks, d_ff, final_norm):
    h = _rmsnorm(x, gain_ref[...]).astype(BF16)
    acc = None
    c0 = 0
    for ck in chunks:
        g = jnp.dot(h, wgu_ref[:, c0:c0 + ck], preferred_element_type=F32)
        u = jnp.dot(h, wgu_ref[:, d_ff + c0:d_ff + c0 + ck], preferred_element_type=F32)
        a = (jax.nn.silu(g) * u).astype(BF16)
        d = jnp.dot(a, wd_ref[c0:c0 + ck, :], preferred_element_type=F32)
        acc = d if acc is None else acc + d
        c0 += ck
    y = x + 0.5 * acc
    if final_norm:
        y = _rmsnorm(y, gfin_ref[...])
    return y


def _ffn_body(x_ref, xs_ref, gain_ref, wgu_ref, wd_ref, gfin_ref, o_ref, os_ref, wgu_bf, wd_bf,
              *, sub_rows, n_cast, **static):
    step = pl.program_id(0)
    gu_cols = wgu_ref.shape[1]
    down_rows = wd_ref.shape[0]
    for c in range(n_cast):
        @pl.when(step == c)
        def _(c=c):
            wgu_bf[:, c * gu_cols:(c + 1) * gu_cols] = wgu_ref[...].astype(BF16)
            wd_bf[c * down_rows:(c + 1) * down_rows, :] = wd_ref[...].astype(BF16)

    weights = (gain_ref, wgu_bf, wd_bf, gfin_ref)

    @pl.when(step >= n_cast)
    def _():
        for r0 in range(0, x_ref.shape[0], sub_rows):
            o_ref[r0:r0 + sub_rows, :] = _ffn_rows(x_ref[r0:r0 + sub_rows, :], *weights, **static)

    @pl.when(step == pl.num_programs(0) - 1)
    def _():
        os_ref[...] = _ffn_rows(xs_ref[...], *weights, **static)


def _ffn(x2d, xs2d, gain, w_gu, w_down, gain_final, *, layer, final_norm, name):
    n, d = x2d.shape
    ns = xs2d.shape[0]
    d_ff = w_down.shape[1]
    tm = min(FFN_TOKEN_TILE, n)
    sub = min(FFN_SUB_ROWS, tm)
    nc = FFN_CAST_STEPS
    assert n % tm == 0 and tm % sub == 0 and sub % F32_SUBLANES == 0 and ns % F32_SUBLANES == 0
    assert (2 * d_ff) % (nc * 128) == 0 and d_ff % (nc * 16) == 0
    gu_cols, down_rows = 2 * d_ff // nc, d_ff // nc
    chunks = _ff_chunks(d_ff)
    ck = max(chunks)
    est = (2 * (w_gu[0].size + w_down[0].size)
           + 2 * (d * gu_cols + down_rows * d) * 4
           + 4 * tm * d * 4 + 3 * ns * d * 4
           + 2 * sub * (2 * ck * 4 + ck * 2 + d * 2 + 3 * d * 4)
           + (8 << 20))
    body = functools.partial(_ffn_body, sub_rows=sub, n_cast=nc, chunks=chunks, d_ff=d_ff,
                             final_norm=final_norm)
    tile = lambda i: (jnp.maximum(i - nc, 0), 0)
    slab = lambda i: jnp.minimum(i, nc - 1)
    return pl.pallas_call(
        body,
        grid=(nc + n // tm,),
        in_specs=[
            pl.BlockSpec((tm, d), tile),
            _resident((ns, d)),
            _resident((1, d)),
            pl.BlockSpec((None, d, gu_cols), lambda i: (layer, 0, slab(i))),
            pl.BlockSpec((None, down_rows, d), lambda i: (layer, slab(i), 0)),
            _resident((1, d)),
        ],
        out_specs=[pl.BlockSpec((tm, d), tile),
                   pl.BlockSpec((ns, d), lambda i: (0, 0))],
        out_shape=[jax.ShapeDtypeStruct((n, d), F32), jax.ShapeDtypeStruct((ns, d), F32)],
        scratch_shapes=[pltpu.VMEM((d, 2 * d_ff), BF16), pltpu.VMEM((d_ff, d), BF16)],
        compiler_params=pltpu.CompilerParams(
            dimension_semantics=("arbitrary",),
            vmem_limit_bytes=_vmem_limit(est)),
        name=name,
    )(x2d, xs2d, gain, w_gu, w_down, gain_final)


def _pair_scores(q_lo, q_hi, k_band):
    qs = jnp.concatenate([q_lo, q_hi], axis=0)
    return lax.dot_general(qs, k_band, (((1,), (1,)), ((), ())), preferred_element_type=F32)


def _pair_softmax_pv(s, v_band, bias, valid):
    rows = s.shape[0] // 2
    s = s + bias
    if valid is not None:
        s = jnp.where(valid, s, NEG_INF)
    m = jnp.max(s, axis=-1, keepdims=True)
    e = jnp.exp2(s - m)
    o = jnp.dot(e.astype(BF16), v_band, preferred_element_type=F32)
    if v_band.shape[1] == V_EXT_W:
        o = o[:, :PAIR_W] * (1.0 / o[:, PAIR_W:])
    else:
        o = o * (1.0 / jnp.sum(e, axis=-1, keepdims=True))
    lane = lax.broadcasted_iota(jnp.int32, (rows, PAIR_W), 1)
    return jnp.where(lane < HEAD_DIM, o[:rows], o[rows:])


def _split_pair_queries(q):
    lane = lax.broadcasted_iota(jnp.int32, q.shape, 1)
    even = (lane % PAIR_W) < HEAD_DIM
    zero = jnp.zeros_like(q)
    return jnp.where(even, q, zero).astype(BF16), jnp.where(even, zero, q).astype(BF16)


def _short_conv(ubuf_ref, u, w_conv_ref, base, rows):
    ubuf_ref[base:base + rows, :] = u
    y = w_conv_ref[0:1, :] * ubuf_ref[base - 2:base - 2 + rows, :]
    y = y + w_conv_ref[1:2, :] * ubuf_ref[base - 1:base - 1 + rows, :]
    return y + w_conv_ref[2:3, :] * u


def _mixer_prompt_body(x_ref, gain_ref, win_ref, bg_ref, wconv_ref, bias_ref, wco_ref, wao_ref,
                       wo_ref, xo_ref, ko_ref, vo_ref, co_ref,
                       kbuf, vbuf, qlo, qhi, yattn, ubuf, sscr, *, conv_w, n_win):
    w = WINDOW
    d = x_ref.shape[-1]
    t = pl.program_id(1)
    tail_rows = CONV_K - 1
    q0 = 3 * conv_w
    g0 = q0 + 3 * ATTN_W

    @pl.when(t == 0)
    def _():
        kbuf[0:w, :] = jnp.zeros((w, ATTN_W), BF16)
        vlane = lax.broadcasted_iota(jnp.int32, vbuf.shape, 1)
        vbuf[...] = jnp.where((vlane % V_EXT_W) < PAIR_W, 0.0, 1.0).astype(BF16)
        ubuf[0:F32_SUBLANES, :] = jnp.zeros((F32_SUBLANES, conv_w), F32)

    first_valid_row = jnp.where(t > 0, 0, WINDOW)
    col = lax.broadcasted_iota(jnp.int32, (2 * GROUP_Q, GROUP_K), 1)

    def project(wi):
        rows = slice(wi * w, (wi + 1) * w)
        x = x_ref[rows, :]
        h = _rmsnorm(x, gain_ref[...]).astype(BF16)
        zc = jnp.dot(h, win_ref[:, 0:q0], preferred_element_type=F32)
        u = zc[:, conv_w:2 * conv_w] * zc[:, 2 * conv_w:3 * conv_w]
        y = _short_conv(ubuf, u, wconv_ref, F32_SUBLANES + wi * w, w)
        yconv = (zc[:, 0:conv_w] * y).astype(BF16)
        zq = jnp.dot(h, win_ref[:, q0:g0], preferred_element_type=F32)
        k = zq[:, ATTN_W:2 * ATTN_W]
        v = zq[:, 2 * ATTN_W:3 * ATTN_W]
        if wi == n_win - 1:
            ko_ref[...] = k
            vo_ref[...] = v
            co_ref[...] = u[w - tail_rows:w, :]
        kv_rows = slice((wi + 1) * w, (wi + 2) * w)
        kbuf[kv_rows, :] = k.astype(BF16)
        v_bf = v.astype(BF16)
        for j in range(HEAD_PAIRS):
            vbuf[kv_rows, j * V_EXT_W:j * V_EXT_W + PAIR_W] = v_bf[:, j * PAIR_W:(j + 1) * PAIR_W]
        q_even, q_odd = _split_pair_queries(zq[:, 0:ATTN_W] * (LOG2_E * HEAD_DIM ** -0.5))
        qlo[rows, :] = q_even
        qhi[rows, :] = q_odd
        return x, h, yconv

    def scores(item):
        wi, c, j = item
        r0 = wi * w + c * GROUP_Q
        lanes = slice(j * PAIR_W, (j + 1) * PAIR_W)
        return _pair_scores(qlo[r0:r0 + GROUP_Q, lanes], qhi[r0:r0 + GROUP_Q, lanes],
                            kbuf[r0:r0 + GROUP_K, lanes])

    def attention(wi):
        items = [(wi, c, j) for c in range(w // GROUP_Q) for j in range(HEAD_PAIRS)]
        sscr[0] = scores(items[0])
        for i, (_, c, j) in enumerate(items):
            if i + 1 < len(items):
                sscr[(i + 1) % 2] = scores(items[i + 1])
            r0 = wi * w + c * GROUP_Q
            valid = ((col + c * GROUP_Q) >= first_valid_row) if wi == 0 else None
            out = _pair_softmax_pv(sscr[i % 2],
                                   vbuf[r0:r0 + GROUP_K, j * V_EXT_W:(j + 1) * V_EXT_W],
                                   bias_ref[j], valid)
            yattn[r0:r0 + GROUP_Q, j * PAIR_W:(j + 1) * PAIR_W] = out.astype(BF16)

    def merge(wi, x, h, yconv):
        rows = slice(wi * w, (wi + 1) * w)
        gate = jax.nn.sigmoid(jnp.dot(h, win_ref[:, g0:g0 + 2 * d], preferred_element_type=F32)
                              + bg_ref[...])
        mc = jnp.dot(yconv, wco_ref[...], preferred_element_type=F32)
        ma = jnp.dot(yattn[rows, :], wao_ref[...], preferred_element_type=F32)
        m = gate[:, 0:d] * mc + gate[:, d:2 * d] * ma
        xo_ref[rows, :] = x + jnp.dot(m.astype(BF16), wo_ref[...], preferred_element_type=F32)

    for wi in range(n_win):
        projected = project(wi)
        attention(wi)
        merge(wi, *projected)
    kbuf[0:w, :] = kbuf[n_win * w:(n_win + 1) * w, :]
    vbuf[0:w, :] = vbuf[n_win * w:(n_win + 1) * w, :]
    ubuf[F32_SUBLANES - tail_rows:F32_SUBLANES, :] = (
        ubuf[F32_SUBLANES + n_win * w - tail_rows:F32_SUBLANES + n_win * w, :])


def _mixer_prompt(x, gain, w_in, b_gate, w_conv, bias, w_co, w_ao, w_o, *, layer, name):
    b, s, d = x.shape
    conv_w = w_conv.shape[-1]
    n_win = MIX_WINDOWS_PER_STEP
    tm = n_win * WINDOW
    assert s % tm == 0 and conv_w % 128 == 0
    weights = 2 * (w_in[0].size + w_co[0].size + w_ao[0].size + w_o[0].size) + 4 * bias.size
    est = (weights
           + 4 * tm * d * 4
           + 4 * WINDOW * ATTN_W * 4
           + (3 * (tm + WINDOW) + 3 * tm) * ATTN_W * 2
           + 2 * 2 * GROUP_Q * GROUP_K * 4
           + (tm + F32_SUBLANES) * conv_w * 4
           + WINDOW * (3 * conv_w + 3 * ATTN_W + 4 * d + 3 * d) * 4
           + (8 << 20))
    row = lambda bi, ti: (bi, ti, 0)
    per_batch = lambda bi, ti: (bi, 0, 0)
    body = functools.partial(_mixer_prompt_body, conv_w=conv_w, n_win=n_win)
    return pl.pallas_call(
        body,
        grid=(b, s // tm),
        in_specs=[
            pl.BlockSpec((None, tm, d), row),
            _resident((1, d)),
            _layer_resident(w_in, layer),
            _resident(b_gate.shape),
            _resident(w_conv.shape),
            _resident(bias.shape),
            _layer_resident(w_co, layer),
            _layer_resident(w_ao, layer),
            _layer_resident(w_o, layer),
        ],
        out_specs=[
            pl.BlockSpec((None, tm, d), row),
            pl.BlockSpec((None, WINDOW, ATTN_W), per_batch),
            pl.BlockSpec((None, WINDOW, ATTN_W), per_batch),
            pl.BlockSpec((None, CONV_K - 1, conv_w), per_batch),
        ],
        out_shape=[
            jax.ShapeDtypeStruct((b, s, d), F32),
            jax.ShapeDtypeStruct((b, WINDOW, ATTN_W), F32),
            jax.ShapeDtypeStruct((b, WINDOW, ATTN_W), F32),
            jax.ShapeDtypeStruct((b, CONV_K - 1, conv_w), F32),
        ],
        scratch_shapes=[
            pltpu.VMEM((tm + WINDOW, ATTN_W), BF16),
            pltpu.VMEM((tm + WINDOW, HEAD_PAIRS * V_EXT_W), BF16),
            pltpu.VMEM((tm, ATTN_W), BF16),
            pltpu.VMEM((tm, ATTN_W), BF16),
            pltpu.VMEM((tm, ATTN_W), BF16),
            pltpu.VMEM((tm + F32_SUBLANES, conv_w), F32),
            pltpu.VMEM((2, 2 * GROUP_Q, GROUP_K), F32),
        ],
        compiler_params=pltpu.CompilerParams(
            dimension_semantics=("arbitrary", "arbitrary"),
            vmem_limit_bytes=_vmem_limit(est)),
        name=name,
    )(x, gain, w_in, b_gate, w_conv, bias, w_co, w_ao, w_o)


def _mixer_sample_body(x_ref, gain_ref, win_ref, bg_ref, wconv_ref, bias_ref, wco_ref, wao_ref,
                       wo_ref, ck_ref, cv_ref, st_ref, xo_ref, ko_ref, vo_ref, co_ref,
                       kband, vband, yconv, yattn, ubuf, *, conv_w, n_streams, t_new, all_valid):
    d = x_ref.shape[-1]
    r_cache = ck_ref.shape[1]
    x = x_ref[...]
    h = _rmsnorm(x, gain_ref[...]).astype(BF16)

    zc = jnp.dot(h, win_ref[:, 0:3 * conv_w], preferred_element_type=F32)
    u = zc[:, conv_w:2 * conv_w] * zc[:, 2 * conv_w:3 * conv_w]
    cb = zc[:, 0:conv_w]
    q0 = 3 * conv_w
    zq = jnp.dot(h, win_ref[:, q0:q0 + 3 * ATTN_W], preferred_element_type=F32)
    k = zq[:, ATTN_W:2 * ATTN_W]
    v = zq[:, 2 * ATTN_W:3 * ATTN_W]
    ko_ref[...] = k
    vo_ref[...] = v
    q_even, q_odd = _split_pair_queries(zq[:, 0:ATTN_W] * (LOG2_E * HEAD_DIM ** -0.5))
    valid = None
    if not all_valid:
        shape = (2 * t_new, r_cache + t_new)
        q_chunk = (PAST_LEN + lax.broadcasted_iota(jnp.int32, shape, 0) % t_new) // CHUNK
        k_pos = PAST_LEN - r_cache + lax.broadcasted_iota(jnp.int32, shape, 1)
        k_chunk = lax.shift_right_arithmetic(k_pos, CHUNK.bit_length() - 1)
        valid = (k_pos >= 0) & (k_chunk <= q_chunk) & (k_chunk >= q_chunk - N_PREV_CHUNKS)

    for bi in range(n_streams):
        rows = slice(bi * t_new, (bi + 1) * t_new)
        ubuf[F32_SUBLANES - (CONV_K - 1):F32_SUBLANES, :] = st_ref[bi]
        y = _short_conv(ubuf, u[rows], wconv_ref, F32_SUBLANES, t_new)
        yconv[rows, :] = (cb[rows] * y).astype(BF16)
        co_ref[bi] = ubuf[pl.ds(F32_SUBLANES + t_new - (CONV_K - 1), CONV_K - 1), :]

        kband[0:r_cache, :] = ck_ref[bi]
        vband[0:r_cache, :] = cv_ref[bi]
        kband[r_cache:r_cache + t_new, :] = k[rows].astype(BF16)
        vband[r_cache:r_cache + t_new, :] = v[rows].astype(BF16)
        for j in range(HEAD_PAIRS):
            lanes = slice(j * PAIR_W, (j + 1) * PAIR_W)
            s = _pair_scores(q_even[rows, lanes], q_odd[rows, lanes], kband[:, lanes])
            out = _pair_softmax_pv(s, vband[:, lanes], bias_ref[j], valid)
            yattn[rows, lanes] = out.astype(BF16)

    g0 = q0 + 3 * ATTN_W
    gate = jax.nn.sigmoid(jnp.dot(h, win_ref[:, g0:g0 + 2 * d], preferred_element_type=F32)
                          + bg_ref[...])
    mc = jnp.dot(yconv[...], wco_ref[...], preferred_element_type=F32)
    ma = jnp.dot(yattn[...], wao_ref[...], preferred_element_type=F32)
    m = gate[:, 0:d] * mc + gate[:, d:2 * d] * ma
    xo_ref[...] = x + jnp.dot(m.astype(BF16), wo_ref[...], preferred_element_type=F32)


def _mixer_sample(x2d, gain, w_in, b_gate, w_conv, bias, w_co, w_ao, w_o, cache_k, cache_v,
                  state, all_valid, *, layer, name):
    n, d = x2d.shape
    _, n_streams, r_cache, _ = cache_k.shape
    t_new = n // n_streams
    conv_w = w_conv.shape[-1]
    n_keys = r_cache + t_new
    assert t_new % 16 == 0 and r_cache % 16 == 0
    stacked = (w_in, w_co, w_ao, w_o, cache_k, cache_v)
    operands = (x2d, gain, w_in, b_gate, w_conv, bias, w_co, w_ao, w_o, cache_k, cache_v, state)
    est = (sum(a[0].size * a.dtype.itemsize if any(a is w for w in stacked)
               else a.size * a.dtype.itemsize for a in operands)
           + 2 * (n * d + 2 * n * ATTN_W + n_streams * (CONV_K - 1) * conv_w) * 4
           + (2 * n_keys + 2 * n) * ATTN_W * 2
           + n * (3 * conv_w + 3 * ATTN_W + 4 * d + 3 * d) * 4
           + (8 << 20))
    body = functools.partial(_mixer_sample_body, conv_w=conv_w, n_streams=n_streams,
                             t_new=t_new, all_valid=all_valid)
    full = lambda shape: pl.BlockSpec(shape, lambda i: (0,) * len(shape))
    return pl.pallas_call(
        body,
        grid=(1,),
        in_specs=[_layer_resident(a, layer) if any(a is w for w in stacked) else _resident(a.shape)
                  for a in operands],
        out_specs=[
            full((n, d)),
            full((n, ATTN_W)),
            full((n, ATTN_W)),
            full((n_streams, CONV_K - 1, conv_w)),
        ],
        out_shape=[
            jax.ShapeDtypeStruct((n, d), F32),
            jax.ShapeDtypeStruct((n, ATTN_W), F32),
            jax.ShapeDtypeStruct((n, ATTN_W), F32),
            jax.ShapeDtypeStruct((n_streams, CONV_K - 1, conv_w), F32),
        ],
        scratch_shapes=[
            pltpu.VMEM((n_keys, ATTN_W), BF16),
            pltpu.VMEM((n_keys, ATTN_W), BF16),
            pltpu.VMEM((n, conv_w), BF16),
            pltpu.VMEM((n, ATTN_W), BF16),
            pltpu.VMEM((t_new + F32_SUBLANES, conv_w), F32),
        ],
        compiler_params=pltpu.CompilerParams(
            dimension_semantics=("arbitrary",),
            vmem_limit_bytes=_vmem_limit(est)),
        name=name,
    )(*operands)


def _band_bias(rel_bias_l, offset, n_q, n_k):
    period = n_k + n_q
    m = np.arange(period)
    m = np.where(m < n_k, m, m - period)
    idx = np.clip(offset - m, -REL_CLIP, REL_CLIP) + REL_CLIP
    vec = rel_bias_l[:, idx]
    h = vec.shape[0]
    flat = jnp.tile(vec, (1, n_q))[:, :n_q * (period - 1)]
    tiles = flat.reshape(h, n_q, period - 1)[:, :, :n_k]
    return tiles.reshape(h // 2, 2 * n_q, n_k)


def _group_bias(rel_bias_l):
    base = _band_bias(rel_bias_l, WINDOW, GROUP_Q, GROUP_K)
    first_key = (np.arange(GROUP_Q)[:, None] // CHUNK) * CHUNK
    key = np.arange(GROUP_K)[None, :]
    in_band = (key >= first_key) & (key < first_key + BAND)
    if in_band.all():
        return base
    return jnp.where(np.concatenate([in_band, in_band], axis=0)[None], base, NEG_INF)


def _band_valid(q_pos, k_pos):
    qc = q_pos[:, None] // CHUNK
    kc = k_pos[None, :] // CHUNK
    return (k_pos[None, :] >= 0) & (kc <= qc) & (kc >= qc - N_PREV_CHUNKS)


def kernel(x_prompt, x_sample, cache_k, cache_v, state_conv, norm_ffn1, w_ffn1_gu, w_ffn1_down,
           norm_mix, w_in, b_gate, w_conv, rel_bias, w_conv_out, w_attn_out, w_o, norm_ffn2,
           w_ffn2_gu, w_ffn2_down, norm_final):
    depth = w_in.shape[0]
    b, s, d = x_prompt.shape
    nb, ts, _ = x_sample.shape
    r_cache = cache_k.shape[2]
    assert min(WINDOW, s) == WINDOW

    xp = x_prompt.reshape(b * s, d)
    xs = x_sample.reshape(nb * ts, d)
    w1gu, w1d, w2gu, w2d = w_ffn1_gu, w_ffn1_down, w_ffn2_gu, w_ffn2_down
    win, wco, wao, wo = (w_in.astype(BF16), w_conv_out.astype(BF16), w_attn_out.astype(BF16),
                         w_o.astype(BF16))
    ck = cache_k.astype(BF16).reshape(depth, nb, r_cache, ATTN_W)
    cv = cache_v.astype(BF16).reshape(depth, nb, r_cache, ATTN_W)
    gfin = norm_final.reshape(1, d)

    qs_pos = PAST_LEN + np.arange(ts)
    ks_pos = np.concatenate([PAST_LEN - r_cache + np.arange(r_cache), qs_pos])
    all_valid_s = bool(_band_valid(qs_pos, ks_pos).all())

    outs = {n: [] for n in ("kp", "vp", "cp", "ks", "vs", "cs")}
    for l in range(depth):
        last = l == depth - 1
        g1 = norm_ffn1[l].reshape(1, d)
        gm = norm_mix[l].reshape(1, d)
        g2 = norm_ffn2[l].reshape(1, d)
        bg = b_gate[l].reshape(1, 2 * d)
        bias_p = _group_bias(rel_bias[l] * LOG2_E)
        bias_s = _band_bias(rel_bias[l] * LOG2_E, r_cache, ts, r_cache + ts)

        xp, xs = _ffn(xp, xs, g1, w1gu, w1d, gfin, layer=l, final_norm=False, name=f"ffn1_l{l}")

        xp3, kp, vp, cp = _mixer_prompt(xp.reshape(b, s, d), gm, win, bg, w_conv[l], bias_p,
                                        wco, wao, wo, layer=l, name=f"mixer_prompt_l{l}")
        xp = xp3.reshape(b * s, d)
        xs, ks, vs, cs = _mixer_sample(xs, gm, win, bg, w_conv[l], bias_s, wco, wao, wo, ck, cv,
                                       state_conv[l], all_valid_s, layer=l,
                                       name=f"mixer_sample_l{l}")

        xp, xs = _ffn(xp, xs, g2, w2gu, w2d, gfin, layer=l, final_norm=last, name=f"ffn2_l{l}")

        outs["kp"].append(kp.reshape(b, WINDOW, N_HEADS, HEAD_DIM))
        outs["vp"].append(vp.reshape(b, WINDOW, N_HEADS, HEAD_DIM))
        outs["cp"].append(cp)
        outs["ks"].append(ks.reshape(nb, ts, N_HEADS, HEAD_DIM))
        outs["vs"].append(vs.reshape(nb, ts, N_HEADS, HEAD_DIM))
        outs["cs"].append(cs)

    stack = lambda name: jnp.stack(outs[name], axis=0)
    return (xp.reshape(b, s, d), xs.reshape(nb, ts, d), stack("kp"), stack("vp"), stack("cp"),
            stack("ks"), stack("vs"), stack("cs"))
```
